```python
import math
import jax, jax.numpy as jnp
from jax import lax
import numpy as np

D_MODEL = 1024
BATCH = 2
SEQ = 8192
DEPTH = 2
DEC_BATCH = 128
DEC_SEQ = 1
PAST_LEN = 8192
PAGE_SIZE = 128

CONV_WIDTH = 4
SSD_HEADS = 16
SSD_HEAD_DIM = 64
SSD_WIDTH = SSD_HEADS * SSD_HEAD_DIM
SSD_GROUPS = 2
SSD_HPG = SSD_HEADS // SSD_GROUPS
SSD_STATE = 64
SSD_GN = SSD_GROUPS * SSD_STATE
SSD_CONV_DIM = SSD_WIDTH + 2 * SSD_GN
SSD_CHUNK = 128
DT_MIN = 1e-3
DT_MAX = 1e-1
LRU_WIDTH = 512
LRU_BLOCKS = 8
LRU_BLOCK_DIM = LRU_WIDTH // LRU_BLOCKS
LRU_C = 8.0
ATTN_Q_HEADS = 8
ATTN_KV_HEADS = 2
ATTN_GROUP = ATTN_Q_HEADS // ATTN_KV_HEADS
ATTN_HEAD_DIM = 64
ATTN_WIDTH = ATTN_Q_HEADS * ATTN_HEAD_DIM
KV_DIM = ATTN_KV_HEADS * ATTN_HEAD_DIM
WINDOW = 128
ATTN_SCALE = ATTN_HEAD_DIM ** -0.5

MIX_WIDTH = SSD_WIDTH + LRU_WIDTH + ATTN_WIDTH
IN_SPLITS = (SSD_WIDTH, SSD_CONV_DIM, SSD_HEADS, LRU_WIDTH, LRU_WIDTH, ATTN_WIDTH, KV_DIM, KV_DIM)
IN_WIDTH = SSD_WIDTH + SSD_CONV_DIM + SSD_HEADS + 2 * LRU_WIDTH + ATTN_WIDTH + 2 * KV_DIM

D_FF_DENSE = 2816
N_EXPERTS = 8
TOP_K = 2
D_FF_EXPERT = 3584
N_DENSE = (DEPTH + 1) // 2
N_MOE = DEPTH // 2

ALPHA = (2.0 * DEPTH) ** 0.25
BETA = (8.0 * DEPTH) ** -0.25
LN_EPS = 1e-5
RMS_EPS = 1e-5

kernel_name = "hybrid_ssd_rglru_swa_deepnorm_step"

F32 = jnp.float32


def split_cols(a, sizes):
    out, start = [], 0
    for s in sizes:
        out.append(a[..., start:start + s])
        start += s
    return out


def layer_norm(x, g, b):
    xf = x.astype(F32)
    mu = jnp.mean(xf, axis=-1, keepdims=True)
    xc = xf - mu
    var = jnp.mean(xc * xc, axis=-1, keepdims=True)
    return (xc * lax.rsqrt(var + LN_EPS) * g.astype(F32) + b.astype(F32)).astype(x.dtype)


def group_rmsnorm(y, g):
    bsz, L, _ = y.shape
    yg = y.reshape(bsz, L, SSD_GROUPS, SSD_WIDTH // SSD_GROUPS)
    yg = yg * lax.rsqrt(jnp.mean(yg * yg, axis=-1, keepdims=True) + RMS_EPS)
    return yg.reshape(bsz, L, SSD_WIDTH) * g.astype(F32)


def causal_depthwise_conv(x, buf, w, b):
    L = x.shape[1]
    xx = jnp.concatenate([buf.astype(x.dtype), x], axis=1)
    out = b + xx[:, 0:L] * w[0]
    for j in range(1, CONV_WIDTH):
        out = out + xx[:, j:j + L] * w[j]
    return out, xx[:, L:]


def ssd_chunked(xs, dt, a, b_in, c_in, h0):
    bsz, L = xs.shape[:2]
    chunk = min(SSD_CHUNK, L)
    nc = -(-L // chunk)
    pad = nc * chunk - L
    xs = xs.astype(F32)
    b_in = b_in.astype(F32)
    c_in = c_in.astype(F32)
    if pad:
        padw = lambda t: jnp.pad(t, [(0, 0), (0, pad)] + [(0, 0)] * (t.ndim - 2))
        xs, dt, b_in, c_in = padw(xs), padw(dt), padw(b_in), padw(c_in)
    rs = lambda t: t.reshape((bsz, nc, chunk) + t.shape[2:])
    xs, dt, b_in, c_in = rs(xs), rs(dt), rs(b_in), rs(c_in)
    acum = jnp.cumsum(dt * a, axis=2)
    xdt = xs * dt[..., None]
    causal = jnp.tril(jnp.ones((chunk, chunk), bool))[:, :, None, None]
    seg = acum[:, :, :, None] - acum[:, :, None, :]
    decay = jnp.exp(jnp.where(causal, seg, -jnp.inf))
    cb = jnp.einsum('bclgn,bcsgn->bclsg', c_in, b_in)
    y_diag = jnp.einsum('bclsg,bclsgj,bcsgjp->bclgjp', cb, decay, xdt)
    decay_end = jnp.exp(acum[:, :, -1:] - acum)
    chunk_states = jnp.einsum('bclgn,bclgj,bclgjp->bcgjpn', b_in, decay_end, xdt)
    chunk_decay = jnp.exp(acum[:, :, -1])

    def carry(h, inp):
        dec, s = inp
        return h * dec[..., None, None] + s, h

    h_last, h_start = lax.scan(carry, h0.astype(F32),
                               (jnp.moveaxis(chunk_decay, 1, 0), jnp.moveaxis(chunk_states, 1, 0)))
    h_start = jnp.moveaxis(h_start, 0, 1)
    y_off = jnp.einsum('bclgn,bcgjpn,bclgj->bclgjp', c_in, h_start, jnp.exp(acum))
    y = (y_diag + y_off).reshape((bsz, nc * chunk) + y_diag.shape[3:])[:, :L]
    return y, h_last


def rglru(xc, h0, wa, ba, wx, bx, lam):
    bsz, L, wd = xc.shape
    xb = xc.reshape(bsz, L, LRU_BLOCKS, LRU_BLOCK_DIM)
    r = jax.nn.sigmoid((jnp.einsum('blki,kij->blkj', xb, wa) + ba).astype(F32)).reshape(bsz, L, wd)
    i = jax.nn.sigmoid((jnp.einsum('blki,kij->blkj', xb, wx) + bx).astype(F32)).reshape(bsz, L, wd)
    log_a = -LRU_C * r * jax.nn.softplus(-lam.astype(F32))
    a = jnp.exp(log_a)
    u = jnp.sqrt(-jnp.expm1(2.0 * log_a)) * (i * xc.astype(F32))

    def combine(left, right):
        a1, b1 = left
        a2, b2 = right
        return a1 * a2, a2 * b1 + b2

    a_cum, b_cum = lax.associative_scan(combine, (a, u), axis=1)
    h = a_cum * h0.astype(F32)[:, None] + b_cum
    return h, h[:, -1]


def sink_softmax(scores, sinks):
    s = scores.astype(F32)
    sink = jnp.broadcast_to(sinks.astype(F32)[:, :, None, None], s.shape[:-1] + (1,))
    return jax.nn.softmax(jnp.concatenate([s, sink], axis=-1), axis=-1)[..., :-1]


def swa_prompt(q, k, v, sinks):
    bsz, L = q.shape[:2]
    nb = -(-L // WINDOW)
    pad = nb * WINDOW - L
    if pad:
        q = jnp.pad(q, [(0, 0), (0, pad), (0, 0), (0, 0), (0, 0)])
        k = jnp.pad(k, [(0, 0), (0, pad), (0, 0), (0, 0)])
        v = jnp.pad(v, [(0, 0), (0, pad), (0, 0), (0, 0)])
    qb = q.reshape(bsz, nb, WINDOW, ATTN_KV_HEADS, ATTN_GROUP, ATTN_HEAD_DIM)
    zeros = jnp.zeros((bsz, WINDOW, ATTN_KV_HEADS, ATTN_HEAD_DIM), k.dtype)
    kb = jnp.concatenate([zeros, k], axis=1).reshape(bsz, nb + 1, WINDOW, ATTN_KV_HEADS, ATTN_HEAD_DIM)
    vb = jnp.concatenate([zeros, v], axis=1).reshape(bsz, nb + 1, WINDOW, ATTN_KV_HEADS, ATTN_HEAD_DIM)
    kband = jnp.concatenate([kb[:, :-1], kb[:, 1:]], axis=2)
    vband = jnp.concatenate([vb[:, :-1], vb[:, 1:]], axis=2)
    scores = jnp.einsum('bnqkgd,bnskd->bnkgqs', qb, kband) * ATTN_SCALE
    qi = jnp.arange(WINDOW)[:, None]
    si = jnp.arange(2 * WINDOW)[None, :]
    rel = qi + WINDOW - si
    band = (rel >= 0) & (rel < WINDOW)
    has_prev = (jnp.arange(nb)[:, None, None] > 0) | (si >= WINDOW)[None]
    mask = band[None] & has_prev
    scores = jnp.where(mask[None, :, None, None], scores, -jnp.inf)
    p = sink_softmax(scores, sinks).astype(v.dtype)
    o = jnp.einsum('bnkgqs,bnskd->bnqkgd', p, vband)
    return o.reshape(bsz, nb * WINDOW, ATTN_WIDTH)[:, :L]


def swa_decode(q, k, v, k_buf, v_buf, sinks):
    T = q.shape[1]
    kk = jnp.concatenate([k_buf.astype(k.dtype), k], axis=1)
    vv = jnp.concatenate([v_buf.astype(v.dtype), v], axis=1)
    scores = jnp.einsum('btkgd,bskd->bkgts', q, kk) * ATTN_SCALE
    rel = jnp.arange(T)[:, None] + WINDOW - jnp.arange(WINDOW + T)[None, :]
    mask = (rel >= 0) & (rel < WINDOW)
    scores = jnp.where(mask, scores, -jnp.inf)
    p = sink_softmax(scores, sinks).astype(v.dtype)
    o = jnp.einsum('bkgts,bskd->btkgd', p, vv)
    return o.reshape(q.shape[0], T, ATTN_WIDTH), kk[:, -WINDOW:], vv[:, -WINDOW:]


def hybrid_mixer(x, ssd_h, ssd_buf, lru_h, lru_buf, k_buf, v_buf, is_prompt, w, l):
    bsz, L, _ = x.shape
    proj = x @ w['w_in'][l]
    z, xbc, dt_raw, lru_gate, lru_in, q, k, v = split_cols(proj, IN_SPLITS)

    xbc, ssd_buf_new = causal_depthwise_conv(xbc, ssd_buf, w['ssd_conv_w'][l], w['ssd_conv_b'][l])
    xbc = jax.nn.silu(xbc)
    xs, b_in, c_in = split_cols(xbc, (SSD_WIDTH, SSD_GN, SSD_GN))
    xs = xs.reshape(bsz, L, SSD_GROUPS, SSD_HPG, SSD_HEAD_DIM)
    b_in = b_in.reshape(bsz, L, SSD_GROUPS, SSD_STATE)
    c_in = c_in.reshape(bsz, L, SSD_GROUPS, SSD_STATE)
    dt = jax.nn.softplus(dt_raw.astype(F32) + w['ssd_dt_bias'][l].astype(F32)).reshape(bsz, L, SSD_GROUPS, SSD_HPG)
    a = -jnp.exp(w['ssd_a_log'][l].astype(F32)).reshape(SSD_GROUPS, SSD_HPG)
    h0 = ssd_h.reshape(bsz, SSD_GROUPS, SSD_HPG, SSD_HEAD_DIM, SSD_STATE)
    y, ssd_h_new = ssd_chunked(xs, dt, a, b_in, c_in, h0)
    y = y + w['ssd_d'][l].astype(F32).reshape(SSD_GROUPS, SSD_HPG)[..., None] * xs.astype(F32)
    y = y.reshape(bsz, L, SSD_WIDTH) * jax.nn.silu(z.astype(F32))
    y_ssd = group_rmsnorm(y, w['ssd_norm_g'][l]).astype(x.dtype)
    ssd_h_new = ssd_h_new.reshape(bsz, SSD_HEADS, SSD_HEAD_DIM, SSD_STATE)

    xl, lru_buf_new = causal_depthwise_conv(lru_in, lru_buf, w['lru_conv_w'][l], w['lru_conv_b'][l])
    h_seq, lru_h_new = rglru(xl, lru_h, w['lru_wa'][l], w['lru_ba'][l], w['lru_wx'][l], w['lru_bx'][l],
                             w['lru_lambda'][l])
    y_lru = h_seq.astype(x.dtype) * jax.nn.gelu(lru_gate)

    q = q.reshape(bsz, L, ATTN_KV_HEADS, ATTN_GROUP, ATTN_HEAD_DIM)
    k = k.reshape(bsz, L, ATTN_KV_HEADS, ATTN_HEAD_DIM)
    v = v.reshape(bsz, L, ATTN_KV_HEADS, ATTN_HEAD_DIM)
    sinks = w['attn_sinks'][l].reshape(ATTN_KV_HEADS, ATTN_GROUP)
    if is_prompt:
        y_attn = swa_prompt(q, k, v, sinks)
        k_new, v_new = k[:, -WINDOW:], v[:, -WINDOW:]
    else:
        y_attn, k_new, v_new = swa_decode(q, k, v, k_buf, v_buf, sinks)

    y_all = jnp.concatenate([y_ssd, y_lru, y_attn], axis=-1)
    out = y_all @ w['w_out'][l]
    return out, (ssd_h_new, ssd_buf_new, lru_h_new, lru_buf_new, k_new, v_new)


def swiglu(x, wg, wu, wd):
    return (jax.nn.silu(x @ wg) * (x @ wu)) @ wd


def moe_ffn(x, router, wg, wu, wd):
    logits = jnp.einsum('bld,de->ble', x, router).astype(F32)
    top_v, top_i = lax.top_k(logits, TOP_K)
    top_g = jax.nn.softmax(top_v, axis=-1)
    gate = jnp.sum(jax.nn.one_hot(top_i, N_EXPERTS, dtype=F32) * top_g[..., None], axis=-2).astype(x.dtype)
    out = jnp.zeros_like(x)
    for e in range(N_EXPERTS):
        out = out + gate[..., e:e + 1] * swiglu(x, wg[e], wu[e], wd[e])
    return out


def run_trunk(x, ssd_h, ssd_buf, lru_h, lru_buf, k_buf, v_buf, is_prompt, w):
    new_states = [[] for _ in range(6)]
    for l in range(DEPTH):
        kb = None if is_prompt else k_buf[l]
        vb = None if is_prompt else v_buf[l]
        mix, st = hybrid_mixer(x, ssd_h[l], ssd_buf[l], lru_h[l], lru_buf[l], kb, vb, is_prompt, w, l)
        for lst, s in zip(new_states, st):
            lst.append(s.astype(x.dtype))
        x = layer_norm(ALPHA * x + mix, w['ln1_g'][l], w['ln1_b'][l])
        if l % 2 == 0:
            i = l // 2
            f = swiglu(x, w['ffn_w_gate'][i], w['ffn_w_up'][i], w['ffn_w_down'][i])
        else:
            i = l // 2
            f = moe_ffn(x, w['moe_router'][i], w['moe_w_gate'][i], w['moe_w_up'][i], w['moe_w_down'][i])
        x = layer_norm(ALPHA * x + f, w['ln2_g'][l], w['ln2_b'][l])
    stacked = [jnp.stack(lst, axis=0) for lst in new_states]
    return x, stacked


def setup_inputs(seed: int = 0) -> dict:
    key = jax.random.key(seed)
    keys = jax.random.split(key, 64)
    counter = [0]

    def nk():
        kk = keys[counter[0]]
        counter[0] += 1
        return kk

    def nrm(shape, scale):
        return jax.random.normal(nk(), shape, F32) * scale

    def uni(shape, lo, hi):
        return jax.random.uniform(nk(), shape, F32, lo, hi)

    dt0 = jnp.exp(uni((DEPTH, SSD_HEADS), math.log(DT_MIN), math.log(DT_MAX)))
    ssd_dt_bias = dt0 + jnp.log(-jnp.expm1(-dt0))
    ssd_a_log = jnp.log(uni((DEPTH, SSD_HEADS), 1.0, 16.0))
    a_root = uni((DEPTH, LRU_WIDTH), 0.9, 0.999) ** (1.0 / LRU_C)
    lru_lambda = jnp.log(a_root) - jnp.log1p(-a_root)

    return {
        'x_prompt': nrm((BATCH, SEQ, D_MODEL), 1.0),
        'x_sample': nrm((DEC_BATCH, DEC_SEQ, D_MODEL), 1.0),
        'state_ssd': nrm((DEPTH, DEC_BATCH, SSD_HEADS, SSD_HEAD_DIM, SSD_STATE), 0.3),
        'state_ssd_conv': nrm((DEPTH, DEC_BATCH, CONV_WIDTH - 1, SSD_CONV_DIM), 1.0),
        'state_lru': nrm((DEPTH, DEC_BATCH, LRU_WIDTH), 0.5),
        'state_lru_conv': nrm((DEPTH, DEC_BATCH, CONV_WIDTH - 1, LRU_WIDTH), 1.0),
        'cache_swa_k': nrm((DEPTH, DEC_BATCH, WINDOW, ATTN_KV_HEADS, ATTN_HEAD_DIM), 1.0),
        'cache_swa_v': nrm((DEPTH, DEC_BATCH, WINDOW, ATTN_KV_HEADS, ATTN_HEAD_DIM), 1.0),
        'w_in': nrm((DEPTH, D_MODEL, IN_WIDTH), D_MODEL ** -0.5),
        'ssd_conv_w': nrm((DEPTH, CONV_WIDTH, SSD_CONV_DIM), CONV_WIDTH ** -0.5),
        'ssd_conv_b': nrm((DEPTH, SSD_CONV_DIM), 0.02),
        'ssd_dt_bias': ssd_dt_bias,
        'ssd_a_log': ssd_a_log,
        'ssd_d': 1.0 + nrm((DEPTH, SSD_HEADS), 0.1),
        'ssd_norm_g': 1.0 + nrm((DEPTH, SSD_WIDTH), 0.02),
        'lru_conv_w': nrm((DEPTH, CONV_WIDTH, LRU_WIDTH), CONV_WIDTH ** -0.5),
        'lru_conv_b': nrm((DEPTH, LRU_WIDTH), 0.02),
        'lru_wa': nrm((DEPTH, LRU_BLOCKS, LRU_BLOCK_DIM, LRU_BLOCK_DIM), LRU_BLOCK_DIM ** -0.5),
        'lru_ba': nrm((DEPTH, LRU_BLOCKS, LRU_BLOCK_DIM), 0.02),
        'lru_wx': nrm((DEPTH, LRU_BLOCKS, LRU_BLOCK_DIM, LRU_BLOCK_DIM), LRU_BLOCK_DIM ** -0.5),
        'lru_bx': nrm((DEPTH, LRU_BLOCKS, LRU_BLOCK_DIM), 0.02),
        'lru_lambda': lru_lambda,
        'attn_sinks': nrm((DEPTH, ATTN_Q_HEADS), 0.5),
        'w_out': nrm((DEPTH, MIX_WIDTH, D_MODEL), BETA * MIX_WIDTH ** -0.5),
        'ln1_g': 1.0 + nrm((DEPTH, D_MODEL), 0.02),
        'ln1_b': nrm((DEPTH, D_MODEL), 0.02),
        'ln2_g': 1.0 + nrm((DEPTH, D_MODEL), 0.02),
        'ln2_b': nrm((DEPTH, D_MODEL), 0.02),
        'ffn_w_gate': nrm((N_DENSE, D_MODEL, D_FF_DENSE), D_MODEL ** -0.5),
        'ffn_w_up': nrm((N_DENSE, D_MODEL, D_FF_DENSE), D_MODEL ** -0.5),
        'ffn_w_down': nrm((N_DENSE, D_FF_DENSE, D_MODEL), BETA * D_FF_DENSE ** -0.5),
        'moe_router': nrm((N_MOE, D_MODEL, N_EXPERTS), D_MODEL ** -0.5),
        'moe_w_gate': nrm((N_MOE, N_EXPERTS, D_MODEL, D_FF_EXPERT), D_MODEL ** -0.5),
        'moe_w_up': nrm((N_MOE, N_EXPERTS, D_MODEL, D_FF_EXPERT), D_MODEL ** -0.5),
        'moe_w_down': nrm((N_MOE, N_EXPERTS, D_FF_EXPERT, D_MODEL), BETA * D_FF_EXPERT ** -0.5),
    }


def reference(x_prompt, x_sample, state_ssd, state_ssd_conv, state_lru, state_lru_conv,
              cache_swa_k, cache_swa_v, w_in, ssd_conv_w, ssd_conv_b, ssd_dt_bias, ssd_a_log,
              ssd_d, ssd_norm_g, lru_conv_w, lru_conv_b, lru_wa, lru_ba, lru_wx, lru_bx,
              lru_lambda, attn_sinks, w_out, ln1_g, ln1_b, ln2_g, ln2_b, ffn_w_gate, ffn_w_up,
              ffn_w_down, moe_router, moe_w_gate, moe_w_up, moe_w_down):
    w = dict(w_in=w_in, ssd_conv_w=ssd_conv_w, ssd_conv_b=ssd_conv_b, ssd_dt_bias=ssd_dt_bias,
             ssd_a_log=ssd_a_log, ssd_d=ssd_d, ssd_norm_g=ssd_norm_g, lru_conv_w=lru_conv_w,
             lru_conv_b=lru_conv_b, lru_wa=lru_wa, lru_ba=lru_ba, lru_wx=lru_wx, lru_bx=lru_bx,
             lru_lambda=lru_lambda, attn_sinks=attn_sinks, w_out=w_out, ln1_g=ln1_g, ln1_b=ln1_b,
             ln2_g=ln2_g, ln2_b=ln2_b, ffn_w_gate=ffn_w_gate, ffn_w_up=ffn_w_up,
             ffn_w_down=ffn_w_down, moe_router=moe_router, moe_w_gate=moe_w_gate,
             moe_w_up=moe_w_up, moe_w_down=moe_w_down)
    bp = x_prompt.shape[0]
    dtp = x_prompt.dtype
    z_ssd = jnp.zeros((DEPTH, bp, SSD_HEADS, SSD_HEAD_DIM, SSD_STATE), dtp)
    z_ssd_conv = jnp.zeros((DEPTH, bp, CONV_WIDTH - 1, SSD_CONV_DIM), dtp)
    z_lru = jnp.zeros((DEPTH, bp, LRU_WIDTH), dtp)
    z_lru_conv = jnp.zeros((DEPTH, bp, CONV_WIDTH - 1, LRU_WIDTH), dtp)
    y_prompt, p_st = run_trunk(x_prompt, z_ssd, z_ssd_conv, z_lru, z_lru_conv, None, None, True, w)
    y_sample, s_st = run_trunk(x_sample, state_ssd, state_ssd_conv, state_lru, state_lru_conv,
                               cache_swa_k, cache_swa_v, False, w)
    p_ssd, p_ssd_conv, p_lru, p_lru_conv, p_k, p_v = p_st
    s_ssd, s_ssd_conv, s_lru, s_lru_conv, s_k, s_v = s_st
    return (y_prompt, y_sample, p_ssd, p_ssd_conv, p_lru, p_lru_conv, p_k, p_v,
            s_ssd, s_ssd_conv, s_lru, s_lru_conv, s_k, s_v)
```

```python
import functools
import math

import jax
import jax.numpy as jnp
from jax import lax
from jax.experimental import pallas as pl
from jax.experimental.pallas import tpu as pltpu

F32 = jnp.float32
BF16 = jnp.bfloat16

D_MODEL = 1024
DEPTH = 2
CONV_WIDTH = 4
SSD_HEADS = 16
SSD_HEAD_DIM = 64
SSD_WIDTH = SSD_HEADS * SSD_HEAD_DIM
SSD_GROUPS = 2
SSD_HPG = SSD_HEADS // SSD_GROUPS
SSD_STATE = 64
SSD_GN = SSD_GROUPS * SSD_STATE
SSD_CONV_DIM = SSD_WIDTH + 2 * SSD_GN
LRU_WIDTH = 512
LRU_BLOCKS = 8
LRU_BLOCK_DIM = LRU_WIDTH // LRU_BLOCKS
LRU_C = 8.0
ATTN_Q_HEADS = 8
ATTN_KV_HEADS = 2
ATTN_GROUP = ATTN_Q_HEADS // ATTN_KV_HEADS
ATTN_HEAD_DIM = 64
ATTN_WIDTH = ATTN_Q_HEADS * ATTN_HEAD_DIM
KV_DIM = ATTN_KV_HEADS * ATTN_HEAD_DIM
WINDOW = 128
ATTN_SCALE = ATTN_HEAD_DIM ** -0.5
MIX_WIDTH = SSD_WIDTH + LRU_WIDTH + ATTN_WIDTH
N_EXPERTS = 8
TOP_K = 2
ALPHA = (2.0 * DEPTH) ** 0.25
LN_EPS = 1e-5
RMS_EPS = 1e-5

LANES = 128
SUBLANES = 8
VMEM_LIMIT_BYTES = 56 * 1024 * 1024

OFF_Z = 0
OFF_XBC = OFF_Z + SSD_WIDTH
OFF_LGATE = OFF_XBC + SSD_CONV_DIM
OFF_LX = OFF_LGATE + LRU_WIDTH
OFF_Q = OFF_LX + LRU_WIDTH
OFF_K = OFF_Q + ATTN_WIDTH
OFF_V = OFF_K + KV_DIM
OFF_DT = OFF_V + KV_DIM
PROJ_WIDTH = OFF_DT + LANES

CHUNK = 128
TOKEN_TILE = 512
PROJ_COL_TILE = PROJ_WIDTH // 3
FF_TILE = 512


def _cparams(*sem):
    return pltpu.CompilerParams(dimension_semantics=sem, vmem_limit_bytes=VMEM_LIMIT_BYTES)


def _dot(a, b):
    return jnp.dot(a.astype(BF16), b.astype(BF16), preferred_element_type=F32)


def _dot_nt(a, b):
    return lax.dot_general(a.astype(BF16), b.astype(BF16), (((1,), (1,)), ((), ())),
                           preferred_element_type=F32)


def _split3(x):
    hi = x.astype(BF16)
    r1 = x - hi.astype(F32)
    mid = r1.astype(BF16)
    lo = (r1 - mid.astype(F32)).astype(BF16)
    return hi, mid, lo


def _select_dot_rhs(x, sel):
    hi, mid, lo = _split3(x)
    d = lambda t: jnp.dot(t, sel, preferred_element_type=F32)
    return d(hi) + d(mid) + d(lo)


def _select_dot_lhs(sel, x):
    hi, mid, lo = _split3(x)
    d = lambda t: jnp.dot(sel, t, preferred_element_type=F32)
    return d(hi) + d(mid) + d(lo)


def _silu(x):
    return x * jax.nn.sigmoid(x)


def _layer_norm(h, g, b):
    mu = jnp.mean(h, axis=-1, keepdims=True)
    hc = h - mu
    var = jnp.mean(hc * hc, axis=-1, keepdims=True)
    return hc * lax.rsqrt(var + LN_EPS) * g + b


def _inproj_kernel(x_ref, w_ref, o_ref):
    o_ref[...] = jnp.dot(x_ref[...].astype(BF16), w_ref[...], preferred_element_type=F32)


def _in_proj(x, w):
    m = x.shape[0]
    tm = min(TOKEN_TILE, m)
    grid = (PROJ_WIDTH // PROJ_COL_TILE, m // tm)
    return pl.pallas_call(
        _inproj_kernel,
        grid=grid,
        in_specs=[pl.BlockSpec((tm, D_MODEL), lambda j, i: (i, 0)),
                  pl.BlockSpec((D_MODEL, PROJ_COL_TILE), lambda j, i: (0, j))],
        out_specs=pl.BlockSpec((tm, PROJ_COL_TILE), lambda j, i: (i, j)),
        out_shape=jax.ShapeDtypeStruct((m, PROJ_WIDTH), F32),
        compiler_params=_cparams("arbitrary", "arbitrary"),
        name="in_proj",
    )(x, w)


def _outproj_ln_kernel(y_ref, x_ref, w_ref, g_ref, b_ref, o_ref):
    mix = jnp.dot(y_ref[...].astype(BF16), w_ref[...], preferred_element_type=F32)
    o_ref[...] = _layer_norm(ALPHA * x_ref[...] + mix, g_ref[...], b_ref[...])


def _out_proj_ln(y_all, x, w, g, b):
    m = x.shape[0]
    tm = min(TOKEN_TILE, m)
    row = lambda i: (i, 0)
    fix = lambda i: (0, 0)
    return pl.pallas_call(
        _outproj_ln_kernel,
        grid=(m // tm,),
        in_specs=[pl.BlockSpec((tm, MIX_WIDTH), row), pl.BlockSpec((tm, D_MODEL), row),
                  pl.BlockSpec((MIX_WIDTH, D_MODEL), fix),
                  pl.BlockSpec((1, D_MODEL), fix), pl.BlockSpec((1, D_MODEL), fix)],
        out_specs=pl.BlockSpec((tm, D_MODEL), row),
        out_shape=jax.ShapeDtypeStruct((m, D_MODEL), F32),
        compiler_params=_cparams("arbitrary"),
        name="out_proj_ln",
    )(y_all, x, w, g, b)


def _swiglu_kernel(te_ref, x_ref, wg_ref, wu_ref, wd_ref, g_ref, b_ref, o_ref, acc_ref, *, fuse_ln):
    f = pl.program_id(1)
    nf = pl.num_programs(1)
    x = x_ref[...]
    xb = x.astype(BF16)
    hg = jnp.dot(xb, wg_ref[0], preferred_element_type=F32)
    hu = jnp.dot(xb, wu_ref[0], preferred_element_type=F32)
    part = jnp.dot((_silu(hg) * hu).astype(BF16), wd_ref[0], preferred_element_type=F32)

    @pl.when(f == 0)
    def _():
        acc_ref[...] = part

    @pl.when(f > 0)
    def _():
        acc_ref[...] += part

    @pl.when(f == nf - 1)
    def _():
        if fuse_ln:
            o_ref[...] = _layer_norm(ALPHA * x + acc_ref[...], g_ref[...], b_ref[...])
        else:
            o_ref[...] = acc_ref[...]


def _swiglu(x, tile_expert, wg, wu, wd, g, b, *, tm, fuse_ln):
    m = x.shape[0]
    assert m % tm == 0 and tile_expert.shape[0] == m // tm
    d_ff = wg.shape[2]
    ff = FF_TILE if d_ff % FF_TILE == 0 else d_ff // 2
    assert d_ff % ff == 0 and ff % LANES == 0
    grid_spec = pltpu.PrefetchScalarGridSpec(
        num_scalar_prefetch=1,
        grid=(m // tm, d_ff // ff),
        in_specs=[pl.BlockSpec((tm, D_MODEL), lambda i, f, te: (i, 0)),
                  pl.BlockSpec((1, D_MODEL, ff), lambda i, f, te: (te[i], 0, f)),
                  pl.BlockSpec((1, D_MODEL, ff), lambda i, f, te: (te[i], 0, f)),
                  pl.BlockSpec((1, ff, D_MODEL), lambda i, f, te: (te[i], f, 0)),
                  pl.BlockSpec((1, D_MODEL), lambda i, f, te: (0, 0)),
                  pl.BlockSpec((1, D_MODEL), lambda i, f, te: (0, 0))],
        out_specs=pl.BlockSpec((tm, D_MODEL), lambda i, f, te: (i, 0)),
        scratch_shapes=[pltpu.VMEM((tm, D_MODEL), F32)],
    )
    return pl.pallas_call(
        functools.partial(_swiglu_kernel, fuse_ln=fuse_ln),
        grid_spec=grid_spec,
        out_shape=jax.ShapeDtypeStruct((m, D_MODEL), F32),
        compiler_params=_cparams("arbitrary", "arbitrary"),
        name="swiglu_ln" if fuse_ln else "swiglu_grouped",
    )(tile_expert, x, wg, wu, wd, g, b)


def _router_kernel(x_ref, wr_ref, idx_ref, gate_ref):
    logits = jnp.dot(x_ref[...].astype(BF16), wr_ref[...], preferred_element_type=F32)
    lane = lax.broadcasted_iota(jnp.int32, logits.shape, 1)
    logits = jnp.where(lane < N_EXPERTS, logits, -jnp.inf)
    v1 = jnp.max(logits, axis=-1, keepdims=True)
    i1 = jnp.min(jnp.where(logits == v1, lane, LANES), axis=-1, keepdims=True)
    rest = jnp.where(lane == i1, -jnp.inf, logits)
    v2 = jnp.max(rest, axis=-1, keepdims=True)
    i2 = jnp.min(jnp.where(rest == v2, lane, LANES), axis=-1, keepdims=True)
    e2 = jnp.exp(v2 - v1)
    den = 1.0 + e2
    idx_ref[...] = jnp.where(lane == 0, i1, jnp.where(lane == 1, i2, 0))
    gate_ref[...] = jnp.where(lane == 0, 1.0 / den, jnp.where(lane == 1, e2 / den, 0.0))


def _router(x, wr):
    m = x.shape[0]
    tm = min(TOKEN_TILE, m)
    row = lambda i: (i, 0)
    return pl.pallas_call(
        _router_kernel,
        grid=(m // tm,),
        in_specs=[pl.BlockSpec((tm, D_MODEL), row), pl.BlockSpec((D_MODEL, LANES), lambda i: (0, 0))],
        out_specs=[pl.BlockSpec((tm, LANES), row), pl.BlockSpec((tm, LANES), row)],
        out_shape=[jax.ShapeDtypeStruct((m, LANES), jnp.int32), jax.ShapeDtypeStruct((m, LANES), F32)],
        compiler_params=_cparams("arbitrary"),
        name="router_top2",
    )(x, wr)


def _row_gather_kernel(idx_ref, src_ref, dst_ref, sem, *, rows):
    base = pl.program_id(0) * rows

    def copy(r, src_row):
        return pltpu.make_async_copy(src_ref.at[pl.ds(src_row, 1)], dst_ref.at[pl.ds(base + r, 1)], sem)

    def start(r, carry):
        copy(r, idx_ref[0, 0, r]).start()
        return carry

    def wait(r, carry):
        copy(r, 0).wait()
        return carry

    lax.fori_loop(0, rows, start, 0)
    lax.fori_loop(0, rows, wait, 0)


def _row_gather(src, idx, rows):
    n_rows = idx.shape[0]
    assert n_rows % rows == 0
    steps = n_rows // rows
    return pl.pallas_call(
        functools.partial(_row_gather_kernel, rows=rows),
        grid=(steps,),
        in_specs=[pl.BlockSpec((1, 1, rows), lambda i: (i, 0, 0), memory_space=pltpu.SMEM),
                  pl.BlockSpec(memory_space=pl.ANY)],
        out_specs=pl.BlockSpec(memory_space=pl.ANY),
        out_shape=jax.ShapeDtypeStruct((n_rows, src.shape[1]), src.dtype),
        scratch_shapes=[pltpu.SemaphoreType.DMA],
        compiler_params=_cparams("arbitrary"),
        name="row_gather",
    )(idx.reshape(steps, 1, rows), src)


def _combine_ln_kernel(x_ref, y_ref, gate_ref, g_ref, b_ref, o_ref):
    gate = gate_ref[...]
    f = gate[:, 0:1] * y_ref[:, :D_MODEL] + gate[:, 1:2] * y_ref[:, D_MODEL:]
    o_ref[...] = _layer_norm(ALPHA * x_ref[...] + f, g_ref[...], b_ref[...])


def _combine_ln(x, y_pairs, gates, g, b):
    m = x.shape[0]
    tm = min(TOKEN_TILE, m)
    row = lambda i: (i, 0)
    fix = lambda i: (0, 0)
    return pl.pallas_call(
        _combine_ln_kernel,
        grid=(m // tm,),
        in_specs=[pl.BlockSpec((tm, D_MODEL), row), pl.BlockSpec((tm, 2 * D_MODEL), row),
                  pl.BlockSpec((tm, LANES), row),
                  pl.BlockSpec((1, D_MODEL), fix), pl.BlockSpec((1, D_MODEL), fix)],
        out_specs=pl.BlockSpec((tm, D_MODEL), row),
        out_shape=jax.ShapeDtypeStruct((m, D_MODEL), F32),
        compiler_params=_cparams("arbitrary"),
        name="moe_combine_ln",
    )(x, y_pairs, gates, g, b)


def _moe_ffn_ln(x, wr, wg, wu, wd, g, b):
    m = x.shape[0]
    tm = min(TOKEN_TILE, m)
    top_i, gates = _router(x, wr)
    e_flat = top_i[:, :TOP_K].reshape(-1)
    onehot = (e_flat[:, None] == jnp.arange(N_EXPERTS, dtype=jnp.int32)[None, :]).astype(jnp.int32)
    rank = jnp.take_along_axis(jnp.cumsum(onehot, axis=0), e_flat[:, None], axis=1)[:, 0] - 1
    counts = jnp.sum(onehot, axis=0)
    padded = ((counts + tm - 1) // tm) * tm
    ends = jnp.cumsum(padded)
    starts = ends - padded
    pos = starts[e_flat] + rank
    n_pad = TOP_K * m + N_EXPERTS * tm
    n_tiles = n_pad // tm
    row_token = jnp.zeros((n_pad,), jnp.int32).at[pos].set(jnp.arange(TOP_K * m, dtype=jnp.int32) // TOP_K)
    tile_start = jnp.arange(n_tiles, dtype=jnp.int32) * tm
    tile_expert = jnp.minimum(jnp.sum((tile_start[:, None] >= ends[None, :]).astype(jnp.int32), axis=1),
                              N_EXPERTS - 1).astype(jnp.int32)
    xs = _row_gather(x, row_token, tm)
    ys = _swiglu(xs, tile_expert, wg, wu, wd, g, b, tm=tm, fuse_ln=False)
    y_pairs = _row_gather(ys, pos.astype(jnp.int32), tm).reshape(m, TOP_K * D_MODEL)
    return _combine_ln(x, y_pairs, gates, g, b)


def _shift_rows(x, tail, sh):
    if sh == 0:
        return x
    row8 = lax.broadcasted_iota(jnp.int32, (SUBLANES, 1), 0)
    xr = pltpu.roll(x, sh, 0)
    head = jnp.where(row8 < sh, pltpu.roll(tail, sh, 0), xr[0:SUBLANES])
    return jnp.concatenate([head, xr[SUBLANES:]], axis=0)


def _causal_conv(x, tail, w_ref, b_ref):
    acc = b_ref[...] + _shift_rows(x, tail, CONV_WIDTH - 1) * w_ref[0:1, :]
    for j in range(1, CONV_WIDTH):
        acc = acc + _shift_rows(x, tail, CONV_WIDTH - 1 - j) * w_ref[j:j + 1, :]
    return acc


def _lru_gates(xl, wax_ref, bax_ref, lam_ref):
    gates = jnp.dot(xl.astype(BF16), wax_ref[...], preferred_element_type=F32) + bax_ref[...]
    r = jax.nn.sigmoid(gates[:, :LRU_WIDTH])
    i = jax.nn.sigmoid(gates[:, LRU_WIDTH:])
    log_a = -LRU_C * r * jax.nn.softplus(-lam_ref[...])
    a = jnp.exp(log_a)
    th = jnp.tanh(log_a)
    u = jnp.sqrt(-2.0 * th / (1.0 - th)) * (i * xl)
    return a, u


def _sink_softmax(s, sink):
    m = jnp.maximum(jnp.max(s, axis=-1, keepdims=True), sink)
    e = jnp.exp(s - m)
    den = jnp.sum(e, axis=-1, keepdims=True) + jnp.exp(sink - m)
    return e, den


def _mixer_prompt_kernel(sink_ref, proj_ref, cw_ref, cb_ref, dtb_ref, alog_ref, drow_ref, ng_ref, rexp_ref,
                         gmask_ref, lcw_ref, lcb_ref, wax_ref, bax_ref, lam_ref,
                         y_ref, s_out_ref, hl_out_ref,
                         xtail, ltail, s_scr, hl_scr, kprev, vprev):
    c = pl.program_id(1)
    nc = pl.num_programs(1)
    t = CHUNK

    @pl.when(c == 0)
    def _():
        xtail[...] = jnp.zeros_like(xtail)
        ltail[...] = jnp.zeros_like(ltail)
        s_scr[...] = jnp.zeros_like(s_scr)
        hl_scr[...] = jnp.zeros_like(hl_scr)
        kprev[...] = jnp.zeros_like(kprev)
        vprev[...] = jnp.zeros_like(vprev)

    row = lax.broadcasted_iota(jnp.int32, (t, 1), 0)
    col = lax.broadcasted_iota(jnp.int32, (1, t), 1)
    lane = lax.broadcasted_iota(jnp.int32, (1, LANES), 1)
    causal = row >= col

    z = proj_ref[:, OFF_Z:OFF_Z + SSD_WIDTH]
    xbc_raw = proj_ref[:, OFF_XBC:OFF_XBC + SSD_CONV_DIM]
    xbc = _silu(_causal_conv(xbc_raw, xtail[...], cw_ref, cb_ref))
    xtail[...] = xbc_raw[t - SUBLANES:t]
    xs = xbc[:, :SSD_WIDTH]
    bm = xbc[:, SSD_WIDTH:SSD_WIDTH + SSD_GN]
    cm = xbc[:, SSD_WIDTH + SSD_GN:]
    dt = jax.nn.softplus(proj_ref[:, OFF_DT:OFF_DT + LANES] + dtb_ref[...])
    da = dt * (-jnp.exp(alog_ref[...]))
    acum = _select_dot_lhs(causal.astype(BF16), da)
    acum_t = acum.T
    rexp = rexp_ref[...]
    acum_e = _select_dot_rhs(acum, rexp)
    dt_e = _select_dot_rhs(dt, rexp)
    xdt = xs * dt_e
    exp_e = jnp.exp(acum_e)
    last_e = acum_e[t - 1:t, :]
    xw = xdt * jnp.exp(last_e - acum_e)
    bmb = bm.astype(BF16)

    pairs = []
    for g in range(SSD_GROUPS):
        gsel = (lane >= g * SSD_STATE) & (lane < (g + 1) * SSD_STATE)
        cb = _dot_nt(jnp.where(gsel, cm, 0.0), bmb)
        for pc in range(SSD_HPG // 2):
            p2 = g * (SSD_HPG // 2) + pc
            ms = []
            for j in (2 * p2, 2 * p2 + 1):
                seg = acum[:, j:j + 1] - acum_t[j:j + 1, :]
                dec = jnp.exp(jnp.where(causal, seg, -jnp.inf))
                ms.append((cb * dec).astype(BF16))
            lhs = jnp.concatenate(ms, axis=1)
            xp = xdt[:, LANES * p2:LANES * (p2 + 1)]
            rhs = jnp.concatenate([jnp.where(lane < SSD_HEAD_DIM, xp, 0.0),
                                   jnp.where(lane >= SSD_HEAD_DIM, xp, 0.0)], axis=0).astype(BF16)
            pairs.append(jnp.dot(lhs, rhs, preferred_element_type=F32))
    y_diag = jnp.concatenate(pairs, axis=1)

    s_old = s_scr[...]
    y_off = _dot(cm, s_old) * exp_e
    btx = _dot(bm.T, xw)
    s_scr[...] = s_old * exp_e[t - 1:t, :] + jnp.where(gmask_ref[...] > 0.0, btx, 0.0)

    y = (y_diag + y_off + drow_ref[...] * xs) * _silu(z)
    half = SSD_WIDTH // SSD_GROUPS
    normed = []
    for g in range(SSD_GROUPS):
        seg = y[:, g * half:(g + 1) * half]
        normed.append(seg * lax.rsqrt(jnp.mean(seg * seg, axis=-1, keepdims=True) + RMS_EPS))
    y_ssd = jnp.concatenate(normed, axis=1) * ng_ref[...]

    lx_raw = proj_ref[:, OFF_LX:OFF_LX + LRU_WIDTH]
    xl = _causal_conv(lx_raw, ltail[...], lcw_ref, lcb_ref)
    ltail[...] = lx_raw[t - SUBLANES:t]
    a_s, u_s = _lru_gates(xl, wax_ref, bax_ref, lam_ref)
    d = 1
    while d < t:
        a_sh = jnp.where(row >= d, pltpu.roll(a_s, d, 0), 1.0)
        u_sh = jnp.where(row >= d, pltpu.roll(u_s, d, 0), 0.0)
        u_s = a_s * u_sh + u_s
        a_s = a_s * a_sh
        d *= 2
    h_seq = a_s * hl_scr[...] + u_s
    hl_scr[...] = h_seq[t - 1:t, :]
    y_lru = h_seq * jax.nn.gelu(proj_ref[:, OFF_LGATE:OFF_LGATE + LRU_WIDTH])

    k = proj_ref[:, OFF_K:OFF_K + KV_DIM]
    v = proj_ref[:, OFF_V:OFF_V + KV_DIM]
    kc = jnp.concatenate([kprev[...], k], axis=0).astype(BF16)
    vc = jnp.concatenate([vprev[...], v], axis=0).astype(BF16)
    kprev[...] = k
    vprev[...] = v
    si = lax.broadcasted_iota(jnp.int32, (1, 2 * t), 1)
    first_key = jnp.where(c > 0, 0, t)
    valid = (si > row) & (si <= row + t) & (si >= first_key)
    blocks = []
    for g in range(ATTN_GROUP):
        qb = proj_ref[:, OFF_Q + LANES * g:OFF_Q + LANES * (g + 1)]
        halves = []
        for kv in range(ATTN_KV_HEADS):
            hsel = (lane >= kv * ATTN_HEAD_DIM) & (lane < (kv + 1) * ATTN_HEAD_DIM)
            s = _dot_nt(jnp.where(hsel, qb, 0.0), kc) * ATTN_SCALE
            s = jnp.where(valid, s, -jnp.inf)
            e, den = _sink_softmax(s, sink_ref[kv * ATTN_GROUP + g])
            halves.append(jnp.dot((e / den).astype(BF16), vc, preferred_element_type=F32))
        blocks.append(jnp.where(lane < ATTN_HEAD_DIM, halves[0], halves[1]))
    y_attn = jnp.concatenate(blocks, axis=1)

    y_ref[...] = jnp.concatenate([y_ssd, y_lru, y_attn], axis=1).astype(y_ref.dtype)

    @pl.when(c == nc - 1)
    def _():
        s_out_ref[0] = s_scr[...]
        hl_out_ref[0] = hl_scr[...]


def _mixer_prompt(proj, lw, bsz, seq):
    nc = seq // CHUNK
    fix = lambda b, c: (0, 0)
    vec = lambda n: pl.BlockSpec((1, n), fix)
    in_specs = [
        pl.BlockSpec(memory_space=pltpu.SMEM),
        pl.BlockSpec((CHUNK, PROJ_WIDTH), lambda b, c: (b * nc + c, 0)),
        pl.BlockSpec((CONV_WIDTH, SSD_CONV_DIM), fix), vec(SSD_CONV_DIM),
        vec(LANES), vec(LANES), vec(SSD_WIDTH), vec(SSD_WIDTH),
        pl.BlockSpec((LANES, SSD_WIDTH), fix), pl.BlockSpec((SSD_GN, SSD_WIDTH), fix),
        pl.BlockSpec((CONV_WIDTH, LRU_WIDTH), fix), vec(LRU_WIDTH),
        pl.BlockSpec((LRU_WIDTH, 2 * LRU_WIDTH), fix), vec(2 * LRU_WIDTH), vec(LRU_WIDTH),
    ]
    out_specs = [
        pl.BlockSpec((CHUNK, MIX_WIDTH), lambda b, c: (b * nc + c, 0)),
        pl.BlockSpec((1, SSD_GN, SSD_WIDTH), lambda b, c: (b, 0, 0)),
        pl.BlockSpec((1, 1, LRU_WIDTH), lambda b, c: (b, 0, 0)),
    ]
    out_shape = [
        jax.ShapeDtypeStruct((bsz * seq, MIX_WIDTH), BF16),
        jax.ShapeDtypeStruct((bsz, SSD_GN, SSD_WIDTH), F32),
        jax.ShapeDtypeStruct((bsz, 1, LRU_WIDTH), F32),
    ]
    scratch = [
        pltpu.VMEM((SUBLANES, SSD_CONV_DIM), F32), pltpu.VMEM((SUBLANES, LRU_WIDTH), F32),
        pltpu.VMEM((SSD_GN, SSD_WIDTH), F32), pltpu.VMEM((1, LRU_WIDTH), F32),
        pltpu.VMEM((CHUNK, KV_DIM), F32), pltpu.VMEM((CHUNK, KV_DIM), F32),
    ]
    return pl.pallas_call(
        _mixer_prompt_kernel,
        grid=(bsz, nc),
        in_specs=in_specs,
        out_specs=out_specs,
        out_shape=out_shape,
        scratch_shapes=scratch,
        compiler_params=_cparams("arbitrary", "arbitrary"),
        name="mixer_prompt",
    )(lw['sinks'], proj, lw['ssd_conv_w'], lw['ssd_conv_b'], lw['dt_bias'], lw['a_log'], lw['d_row'],
      lw['norm_g'], lw['head_expand'], lw['group_mask'], lw['lru_conv_w'], lw['lru_conv_b'],
      lw['lru_wax'], lw['lru_bax'], lw['lru_lambda'])


DEC_ROWS = SUBLANES


def _conv_step(x, buf_ref, w_ref, b_ref):
    acc = b_ref[...] + buf_ref[0] * w_ref[0:1, :]
    for j in range(1, CONV_WIDTH - 1):
        acc = acc + buf_ref[j] * w_ref[j:j + 1, :]
    return acc + x * w_ref[CONV_WIDTH - 1:CONV_WIDTH, :]


def _mixer_decode_kernel(sink_ref, proj_ref, sbuf_ref, lbuf_ref, hl_ref, kc_ref, vc_ref, h_ref,
                         cw_ref, cb_ref, dtb_ref, alog_ref, drow_ref, ng_ref, rexp_ref,
                         lcw_ref, lcb_ref, wax_ref, bax_ref, lam_ref,
                         y_ref, sbuf_out, lbuf_out, hl_out, kc_out, vc_out, h_out):
    nb = DEC_ROWS
    row8 = lax.broadcasted_iota(jnp.int32, (nb, 1), 0)
    lane = lax.broadcasted_iota(jnp.int32, (1, LANES), 1)
    wrow = lax.broadcasted_iota(jnp.int32, (WINDOW, 1), 0)

    z = proj_ref[:, OFF_Z:OFF_Z + SSD_WIDTH]
    xbc_raw = proj_ref[:, OFF_XBC:OFF_XBC + SSD_CONV_DIM]
    xbc = _silu(_conv_step(xbc_raw, sbuf_ref, cw_ref, cb_ref))
    for j in range(CONV_WIDTH - 2):
        sbuf_out[j] = sbuf_ref[j + 1]
    sbuf_out[CONV_WIDTH - 2] = xbc_raw
    xs = xbc[:, :SSD_WIDTH]
    bm = xbc[:, SSD_WIDTH:SSD_WIDTH + SSD_GN]
    cm = xbc[:, SSD_WIDTH + SSD_GN:]
    dt = jax.nn.softplus(proj_ref[:, OFF_DT:OFF_DT + LANES] + dtb_ref[...])
    da = dt * (-jnp.exp(alog_ref[...]))
    rexp = rexp_ref[...]
    decay_e = jnp.exp(_select_dot_rhs(da, rexp))
    xdt = xs * _select_dot_rhs(dt, rexp)
    cols = jnp.concatenate([xdt, decay_e, jnp.zeros((LANES - 2 * nb, SSD_WIDTH), F32)], axis=0)
    cols_t = [cols[:, LANES * p2:LANES * (p2 + 1)].T for p2 in range(SSD_HEADS // 2)]
    bm_g = [bm[:, :SSD_STATE], pltpu.roll(bm, SSD_STATE, 1)[:, :SSD_STATE]]
    cm_g = [cm[:, :SSD_STATE], pltpu.roll(cm, SSD_STATE, 1)[:, :SSD_STATE]]
    half = SSD_WIDTH // SSD_GROUPS
    y_off = [jnp.zeros((nb, half), F32) for _ in range(SSD_GROUPS)]
    for bb in range(nb):
        for g in range(SSD_GROUPS):
            h_g = h_ref[bb, g * SSD_HPG:(g + 1) * SSD_HPG].reshape(half, SSD_STATE)
            y_off[g] = jnp.where(row8 == bb, _dot_nt(cm_g[g], h_g), y_off[g])
        for p2 in range(SSD_HEADS // 2):
            g = p2 // (SSD_HPG // 2)
            h_pair = h_ref[bb, 2 * p2:2 * p2 + 2].reshape(2 * SSD_HEAD_DIM, SSD_STATE)
            xcol = cols_t[p2][:, bb:bb + 1]
            dcol = cols_t[p2][:, nb + bb:nb + bb + 1]
            h_new = h_pair * dcol + xcol * bm_g[g][bb:bb + 1, :]
            h_out[bb, 2 * p2:2 * p2 + 2] = h_new.reshape(2, SSD_HEAD_DIM, SSD_STATE)
    cbv = cm * bm
    ys = []
    for g in range(SSD_GROUPS):
        gsel = (lane >= g * SSD_STATE) & (lane < (g + 1) * SSD_STATE)
        cb_g = jnp.sum(jnp.where(gsel, cbv, 0.0), axis=-1, keepdims=True)
        sl = slice(g * half, (g + 1) * half)
        ys.append(cb_g * xdt[:, sl] + y_off[g] * decay_e[:, sl])
    y = (jnp.concatenate(ys, axis=1) + drow_ref[...] * xs) * _silu(z)
    normed = []
    for g in range(SSD_GROUPS):
        seg = y[:, g * half:(g + 1) * half]
        normed.append(seg * lax.rsqrt(jnp.mean(seg * seg, axis=-1, keepdims=True) + RMS_EPS))
    y_ssd = jnp.concatenate(normed, axis=1) * ng_ref[...]

    lx_raw = proj_ref[:, OFF_LX:OFF_LX + LRU_WIDTH]
    xl = _conv_step(lx_raw, lbuf_ref, lcw_ref, lcb_ref)
    for j in range(CONV_WIDTH - 2):
        lbuf_out[j] = lbuf_ref[j + 1]
    lbuf_out[CONV_WIDTH - 2] = lx_raw
    a_s, u_s = _lru_gates(xl, wax_ref, bax_ref, lam_ref)
    h_lru = a_s * hl_ref[...] + u_s
    hl_out[...] = h_lru
    y_lru = h_lru * jax.nn.gelu(proj_ref[:, OFF_LGATE:OFF_LGATE + LRU_WIDTH])

    k_new = proj_ref[:, OFF_K:OFF_K + KV_DIM]
    v_new = proj_ref[:, OFF_V:OFF_V + KV_DIM]
    head_row_g = row8 // ATTN_KV_HEADS
    head_row_kv = row8 % ATTN_KV_HEADS
    lane_kv = lane // ATTN_HEAD_DIM
    sink_col = jnp.zeros((nb, 1), F32)
    for r in range(ATTN_Q_HEADS):
        sink_col = jnp.where(row8 == r, sink_ref[(r % ATTN_KV_HEADS) * ATTN_GROUP + r // ATTN_KV_HEADS], sink_col)
    y_blocks = [jnp.zeros((nb, LANES), F32) for _ in range(ATTN_GROUP)]
    for bb in range(nb):
        q_rows = jnp.zeros((ATTN_Q_HEADS, LANES), F32)
        for g in range(ATTN_GROUP):
            qb = proj_ref[bb:bb + 1, OFF_Q + LANES * g:OFF_Q + LANES * (g + 1)]
            q_rows = jnp.where((head_row_g == g) & (head_row_kv == lane_kv), qb, q_rows)
        kb = kc_ref[bb]
        vb = vc_ref[bb]
        kn = k_new[bb:bb + 1, :]
        vn = v_new[bb:bb + 1, :]
        s = _dot_nt(q_rows, kb) * ATTN_SCALE
        s = jnp.where(lane >= 1, s, -jnp.inf)
        s_new = jnp.sum(q_rows * kn, axis=-1, keepdims=True) * ATTN_SCALE
        m = jnp.maximum(jnp.maximum(jnp.max(s, axis=-1, keepdims=True), s_new), sink_col)
        e = jnp.exp(s - m)
        e_new = jnp.exp(s_new - m)
        den = jnp.sum(e, axis=-1, keepdims=True) + e_new + jnp.exp(sink_col - m)
        o = _dot(e / den, vb) + (e_new / den) * vn
        for g in range(ATTN_GROUP):
            blk = jnp.where(lane < ATTN_HEAD_DIM, o[2 * g:2 * g + 1, :], o[2 * g + 1:2 * g + 2, :])
            y_blocks[g] = jnp.where(row8 == bb, blk, y_blocks[g])
        kc_out[bb] = jnp.where(wrow == WINDOW - 1, kn, pltpu.roll(kb, WINDOW - 1, 0))
        vc_out[bb] = jnp.where(wrow == WINDOW - 1, vn, pltpu.roll(vb, WINDOW - 1, 0))
    y_attn = jnp.concatenate(y_blocks, axis=1)

    y_ref[...] = jnp.concatenate([y_ssd, y_lru, y_attn], axis=1)


def _mixer_decode(proj, lw, sbuf, lbuf, hl, kc, vc, h):
    bsz = proj.shape[0]
    nb = DEC_ROWS
    fix = lambda i: (0, 0)
    vec = lambda n: pl.BlockSpec((1, n), fix)
    rows = lambda n: pl.BlockSpec((nb, n), lambda i: (i, 0))
    buf = lambda n: pl.BlockSpec((CONV_WIDTH - 1, nb, n), lambda i: (0, i, 0))
    cache = pl.BlockSpec((nb, WINDOW, KV_DIM), lambda i: (i, 0, 0))
    state = pl.BlockSpec((nb, SSD_HEADS, SSD_HEAD_DIM, SSD_STATE), lambda i: (i, 0, 0, 0))
    in_specs = [
        pl.BlockSpec(memory_space=pltpu.SMEM),
        rows(PROJ_WIDTH), buf(SSD_CONV_DIM), buf(LRU_WIDTH), rows(LRU_WIDTH), cache, cache, state,
        pl.BlockSpec((CONV_WIDTH, SSD_CONV_DIM), fix), vec(SSD_CONV_DIM),
        vec(LANES), vec(LANES), vec(SSD_WIDTH), vec(SSD_WIDTH),
        pl.BlockSpec((LANES, SSD_WIDTH), fix),
        pl.BlockSpec((CONV_WIDTH, LRU_WIDTH), fix), vec(LRU_WIDTH),
        pl.BlockSpec((LRU_WIDTH, 2 * LRU_WIDTH), fix), vec(2 * LRU_WIDTH), vec(LRU_WIDTH),
    ]
    out_specs = [rows(MIX_WIDTH), buf(SSD_CONV_DIM), buf(LRU_WIDTH), rows(LRU_WIDTH), cache, cache, state]
    out_shape = [
        jax.ShapeDtypeStruct((bsz, MIX_WIDTH), F32),
        jax.ShapeDtypeStruct(sbuf.shape, F32), jax.ShapeDtypeStruct(lbuf.shape, F32),
        jax.ShapeDtypeStruct(hl.shape, F32), jax.ShapeDtypeStruct(kc.shape, F32),
        jax.ShapeDtypeStruct(vc.shape, F32), jax.ShapeDtypeStruct(h.shape, F32),
    ]
    return pl.pallas_call(
        _mixer_decode_kernel,
        grid=(bsz // nb,),
        in_specs=in_specs,
        out_specs=out_specs,
        out_shape=out_shape,
        compiler_params=_cparams("arbitrary"),
        name="mixer_decode",
    )(lw['sinks'], proj, sbuf, lbuf, hl, kc, vc, h,
      lw['ssd_conv_w'], lw['ssd_conv_b'], lw['dt_bias'], lw['a_log'], lw['d_row'], lw['norm_g'],
      lw['head_expand'], lw['lru_conv_w'], lw['lru_conv_b'], lw['lru_wax'], lw['lru_bax'], lw['lru_lambda'])


def _q_perm():
    g, kv, d = jnp.meshgrid(jnp.arange(ATTN_GROUP), jnp.arange(ATTN_KV_HEADS), jnp.arange(ATTN_HEAD_DIM),
                            indexing='ij')
    return (kv * ATTN_GROUP * ATTN_HEAD_DIM + g * ATTN_HEAD_DIM + d).reshape(-1)


def _block_diag(w):
    eye = jnp.eye(LRU_BLOCKS, dtype=w.dtype)
    return jnp.einsum('kij,kl->kilj', w, eye).reshape(LRU_WIDTH, LRU_WIDTH)


def _prep_layer(l, p):
    w_in = p['w_in'][l]
    o = 0
    segs = {}
    for name, size in (('z', SSD_WIDTH), ('xbc', SSD_CONV_DIM), ('dt', SSD_HEADS), ('lgate', LRU_WIDTH),
                       ('lx', LRU_WIDTH), ('q', ATTN_WIDTH), ('k', KV_DIM), ('v', KV_DIM)):
        segs[name] = w_in[:, o:o + size]
        o += size
    qp = _q_perm()
    w_in_p = jnp.concatenate(
        [segs['z'], segs['xbc'], segs['lgate'], segs['lx'], segs['q'][:, qp], segs['k'], segs['v'],
         jnp.pad(segs['dt'], ((0, 0), (0, LANES - SSD_HEADS)))], axis=1).astype(BF16)
    w_out = p['w_out'][l]
    attn_rows = w_out[SSD_WIDTH + LRU_WIDTH:]
    w_out_p = jnp.concatenate([w_out[:SSD_WIDTH + LRU_WIDTH], attn_rows[qp]], axis=0).astype(BF16)
    pad_h = lambda v: jnp.pad(v, (0, LANES - SSD_HEADS)).reshape(1, LANES)
    head_of_lane = jnp.arange(SSD_WIDTH) // SSD_HEAD_DIM
    head_expand = (jnp.arange(LANES)[:, None] == head_of_lane[None, :]).astype(BF16)
    group_mask = ((jnp.arange(SSD_GN)[:, None] // SSD_STATE) == (head_of_lane[None, :] // SSD_HPG)).astype(F32)
    return dict(
        w_in=w_in_p, w_out=w_out_p,
        ssd_conv_w=p['ssd_conv_w'][l], ssd_conv_b=p['ssd_conv_b'][l].reshape(1, -1),
        dt_bias=pad_h(p['ssd_dt_bias'][l]), a_log=pad_h(p['ssd_a_log'][l]),
        d_row=jnp.repeat(p['ssd_d'][l], SSD_HEAD_DIM).reshape(1, -1),
        norm_g=p['ssd_norm_g'][l].reshape(1, -1),
        head_expand=head_expand, group_mask=group_mask,
        lru_conv_w=p['lru_conv_w'][l], lru_conv_b=p['lru_conv_b'][l].reshape(1, -1),
        lru_wax=jnp.concatenate([_block_diag(p['lru_wa'][l]), _block_diag(p['lru_wx'][l])], axis=1).astype(BF16),
        lru_bax=jnp.concatenate([p['lru_ba'][l].reshape(-1), p['lru_bx'][l].reshape(-1)]).reshape(1, -1),
        lru_lambda=p['lru_lambda'][l].reshape(1, -1),
        sinks=p['attn_sinks'][l],
        channel=_prep_channel(l, p),
        ln1_g=p['ln1_g'][l].reshape(1, -1), ln1_b=p['ln1_b'][l].reshape(1, -1),
        ln2_g=p['ln2_g'][l].reshape(1, -1), ln2_b=p['ln2_b'][l].reshape(1, -1),
    )


def _prep_channel(l, p):
    i = l // 2
    if l % 2 == 0:
        return dict(wg=p['ffn_w_gate'][i][None].astype(BF16), wu=p['ffn_w_up'][i][None].astype(BF16),
                    wd=p['ffn_w_down'][i][None].astype(BF16))
    return dict(wr=jnp.pad(p['moe_router'][i], ((0, 0), (0, LANES - N_EXPERTS))).astype(BF16),
                wg=p['moe_w_gate'][i].astype(BF16), wu=p['moe_w_up'][i].astype(BF16),
                wd=p['moe_w_down'][i].astype(BF16))


def _channel_mixer(l, x, lw):
    cw = lw['channel']
    if l % 2 == 0:
        tm = min(TOKEN_TILE, x.shape[0])
        zero_tiles = jnp.zeros((x.shape[0] // tm,), jnp.int32)
        return _swiglu(x, zero_tiles, cw['wg'], cw['wu'], cw['wd'], lw['ln2_g'], lw['ln2_b'], tm=tm, fuse_ln=True)
    return _moe_ffn_ln(x, cw['wr'], cw['wg'], cw['wu'], cw['wd'], lw['ln2_g'], lw['ln2_b'])


def _ssd_state_from_scratch_layout(s):
    bsz = s.shape[0]
    s6 = s.reshape(bsz, SSD_GROUPS, SSD_STATE, SSD_GROUPS, SSD_HPG, SSD_HEAD_DIM)
    diag = jnp.stack([s6[:, g, :, g] for g in range(SSD_GROUPS)], axis=1)
    return jnp.transpose(diag, (0, 1, 3, 4, 2)).reshape(bsz, SSD_HEADS, SSD_HEAD_DIM, SSD_STATE)


def _prompt_trunk(x_prompt, lws, p):
    bsz, seq, _ = x_prompt.shape
    x = x_prompt.reshape(bsz * seq, D_MODEL)
    states = [[] for _ in range(6)]
    for l in range(DEPTH):
        lw = lws[l]
        proj = _in_proj(x, lw['w_in'])
        y_all, s_fin, hl_fin = _mixer_prompt(proj, lw, bsz, seq)
        proj3 = proj.reshape(bsz, seq, PROJ_WIDTH)
        states[0].append(_ssd_state_from_scratch_layout(s_fin))
        states[1].append(proj3[:, seq - (CONV_WIDTH - 1):, OFF_XBC:OFF_XBC + SSD_CONV_DIM])
        states[2].append(hl_fin.reshape(bsz, LRU_WIDTH))
        states[3].append(proj3[:, seq - (CONV_WIDTH - 1):, OFF_LX:OFF_LX + LRU_WIDTH])
        states[4].append(proj3[:, seq - WINDOW:, OFF_K:OFF_K + KV_DIM].reshape(bsz, WINDOW, ATTN_KV_HEADS, ATTN_HEAD_DIM))
        states[5].append(proj3[:, seq - WINDOW:, OFF_V:OFF_V + KV_DIM].reshape(bsz, WINDOW, ATTN_KV_HEADS, ATTN_HEAD_DIM))
        x = _out_proj_ln(y_all, x, lw['w_out'], lw['ln1_g'], lw['ln1_b'])
        x = _channel_mixer(l, x, lw)
    return x.reshape(bsz, seq, D_MODEL), [jnp.stack(s, axis=0) for s in states]


def _sample_trunk(x_sample, lws, p, state_ssd, state_ssd_conv, state_lru, state_lru_conv, cache_k, cache_v):
    bsz = x_sample.shape[0]
    x = x_sample.reshape(bsz, D_MODEL)
    states = [[] for _ in range(6)]
    for l in range(DEPTH):
        lw = lws[l]
        proj = _in_proj(x, lw['w_in'])
        y_all, sbuf, lbuf, hl, kc, vc, h = _mixer_decode(
            proj, lw, jnp.transpose(state_ssd_conv[l], (1, 0, 2)), jnp.transpose(state_lru_conv[l], (1, 0, 2)),
            state_lru[l], cache_k[l].reshape(bsz, WINDOW, KV_DIM), cache_v[l].reshape(bsz, WINDOW, KV_DIM),
            state_ssd[l])
        states[0].append(h)
        states[1].append(jnp.transpose(sbuf, (1, 0, 2)))
        states[2].append(hl)
        states[3].append(jnp.transpose(lbuf, (1, 0, 2)))
        states[4].append(kc.reshape(bsz, WINDOW, ATTN_KV_HEADS, ATTN_HEAD_DIM))
        states[5].append(vc.reshape(bsz, WINDOW, ATTN_KV_HEADS, ATTN_HEAD_DIM))
        x = _out_proj_ln(y_all, x, lw['w_out'], lw['ln1_g'], lw['ln1_b'])
        x = _channel_mixer(l, x, lw)
    return x.reshape(bsz, 1, D_MODEL), [jnp.stack(s, axis=0) for s in states]


def kernel(x_prompt, x_sample, state_ssd, state_ssd_conv, state_lru, state_lru_conv, cache_swa_k, cache_swa_v, w_in, ssd_conv_w, ssd_conv_b, ssd_dt_bias, ssd_a_log, ssd_d, ssd_norm_g, lru_conv_w, lru_conv_b, lru_wa, lru_ba, lru_wx, lru_bx, lru_lambda, attn_sinks, w_out, ln1_g, ln1_b, ln2_g, ln2_b, ffn_w_gate, ffn_w_up, ffn_w_down, moe_router, moe_w_gate, moe_w_up, moe_w_down):
    p = dict(w_in=w_in, ssd_conv_w=ssd_conv_w, ssd_conv_b=ssd_conv_b, ssd_dt_bias=ssd_dt_bias,
             ssd_a_log=ssd_a_log, ssd_d=ssd_d, ssd_norm_g=ssd_norm_g, lru_conv_w=lru_conv_w,
             lru_conv_b=lru_conv_b, lru_wa=lru_wa, lru_ba=lru_ba, lru_wx=lru_wx, lru_bx=lru_bx,
             lru_lambda=lru_lambda, attn_sinks=attn_sinks, w_out=w_out, ln1_g=ln1_g, ln1_b=ln1_b,
             ln2_g=ln2_g, ln2_b=ln2_b, ffn_w_gate=ffn_w_gate, ffn_w_up=ffn_w_up,
             ffn_w_down=ffn_w_down, moe_router=moe_router, moe_w_gate=moe_w_gate,
             moe_w_up=moe_w_up, moe_w_down=moe_w_down)
    lws = [_prep_layer(l, p) for l in range(DEPTH)]
    y_prompt, p_st = _prompt_trunk(x_prompt, lws, p)
    y_sample, s_st = _sample_trunk(x_sample, lws, p, state_ssd, state_ssd_conv, state_lru, state_lru_conv,
                                   cache_swa_k, cache_swa_v)
    return (y_prompt, y_sample) + tuple(p_st) + tuple(s_st)
```

```python
import functools
import math

import jax
import jax.numpy as jnp
from jax import lax
from jax.experimental import pallas as pl
from jax.experimental.pallas import tpu as pltpu

F32 = jnp.float32
BF16 = jnp.bfloat16

D_MODEL = 1024
DEPTH = 2
CONV_WIDTH = 4
SSD_HEADS = 16
SSD_HEAD_DIM = 64
SSD_WIDTH = SSD_HEADS * SSD_HEAD_DIM
SSD_GROUPS = 2
SSD_HPG = SSD_HEADS // SSD_GROUPS
SSD_STATE = 64
SSD_GN = SSD_GROUPS * SSD_STATE
SSD_CONV_DIM = SSD_WIDTH + 2 * SSD_GN
LRU_WIDTH = 512
LRU_BLOCKS = 8
LRU_BLOCK_DIM = LRU_WIDTH // LRU_BLOCKS
LRU_C = 8.0
ATTN_Q_HEADS = 8
ATTN_KV_HEADS = 2
ATTN_GROUP = ATTN_Q_HEADS // ATTN_KV_HEADS
ATTN_HEAD_DIM = 64
ATTN_WIDTH = ATTN_Q_HEADS * ATTN_HEAD_DIM
KV_DIM = ATTN_KV_HEADS * ATTN_HEAD_DIM
WINDOW = 128
ATTN_SCALE = ATTN_HEAD_DIM ** -0.5
MIX_WIDTH = SSD_WIDTH + LRU_WIDTH + ATTN_WIDTH
N_EXPERTS = 8
TOP_K = 2
ALPHA = (2.0 * DEPTH) ** 0.25
LN_EPS = 1e-5
RMS_EPS = 1e-5

LANES = 128
SUBLANES = 8
VMEM_LIMIT_BYTES = 56 * 1024 * 1024

OFF_Z = 0
OFF_XBC = OFF_Z + SSD_WIDTH
OFF_LGATE = OFF_XBC + SSD_CONV_DIM
OFF_LX = OFF_LGATE + LRU_WIDTH
OFF_Q = OFF_LX + LRU_WIDTH
OFF_K = OFF_Q + ATTN_WIDTH
OFF_V = OFF_K + KV_DIM
OFF_DT = OFF_V + KV_DIM
PROJ_WIDTH = OFF_DT + LANES

CHUNK = 128
TOKEN_TILE = 512
PROJ_COL_TILE = PROJ_WIDTH // 3
FF_TILE = 512
TOKEN_TILE_SHAPE = (SUBLANES, LANES)
assert SUBLANES * LANES == D_MODEL


def _cparams(*sem):
    return pltpu.CompilerParams(dimension_semantics=sem, vmem_limit_bytes=VMEM_LIMIT_BYTES)


def _dot(a, b):
    return jnp.dot(a.astype(BF16), b.astype(BF16), preferred_element_type=F32)


def _dot_nt(a, b):
    return lax.dot_general(a.astype(BF16), b.astype(BF16), (((1,), (1,)), ((), ())),
                           preferred_element_type=F32)


def _split3(x):
    hi = x.astype(BF16)
    r1 = x - hi.astype(F32)
    mid = r1.astype(BF16)
    lo = (r1 - mid.astype(F32)).astype(BF16)
    return hi, mid, lo


def _select_dot_rhs(x, sel):
    hi, mid, lo = _split3(x)
    d = lambda t: jnp.dot(t, sel, preferred_element_type=F32)
    return d(hi) + d(mid) + d(lo)


def _select_dot_lhs(sel, x):
    hi, mid, lo = _split3(x)
    d = lambda t: jnp.dot(sel, t, preferred_element_type=F32)
    return d(hi) + d(mid) + d(lo)


def _silu(x):
    return x * jax.nn.sigmoid(x)


def _layer_norm(h, g, b):
    mu = jnp.mean(h, axis=-1, keepdims=True)
    hc = h - mu
    var = jnp.mean(hc * hc, axis=-1, keepdims=True)
    return hc * lax.rsqrt(var + LN_EPS) * g + b


def _inproj_kernel(x_ref, w_ref, o_ref):
    o_ref[...] = jnp.dot(x_ref[...].astype(BF16), w_ref[...], preferred_element_type=F32)


def _in_proj(x, w):
    m = x.shape[0]
    tm = min(TOKEN_TILE, m)
    grid = (PROJ_WIDTH // PROJ_COL_TILE, m // tm)
    return pl.pallas_call(
        _inproj_kernel,
        grid=grid,
        in_specs=[pl.BlockSpec((tm, D_MODEL), lambda j, i: (i, 0)),
                  pl.BlockSpec((D_MODEL, PROJ_COL_TILE), lambda j, i: (0, j))],
        out_specs=pl.BlockSpec((tm, PROJ_COL_TILE), lambda j, i: (i, j)),
        out_shape=jax.ShapeDtypeStruct((m, PROJ_WIDTH), F32),
        compiler_params=_cparams("arbitrary", "arbitrary"),
        name="in_proj",
    )(x, w)


def _outproj_ln_kernel(y_ref, x_ref, w_ref, g_ref, b_ref, o_ref):
    mix = jnp.dot(y_ref[...].astype(BF16), w_ref[...], preferred_element_type=F32)
    o_ref[...] = _layer_norm(ALPHA * x_ref[...] + mix, g_ref[...], b_ref[...])


def _out_proj_ln(y_all, x, w, g, b):
    m = x.shape[0]
    tm = min(TOKEN_TILE, m)
    row = lambda i: (i, 0)
    fix = lambda i: (0, 0)
    return pl.pallas_call(
        _outproj_ln_kernel,
        grid=(m // tm,),
        in_specs=[pl.BlockSpec((tm, MIX_WIDTH), row), pl.BlockSpec((tm, D_MODEL), row),
                  pl.BlockSpec((MIX_WIDTH, D_MODEL), fix),
                  pl.BlockSpec((1, D_MODEL), fix), pl.BlockSpec((1, D_MODEL), fix)],
        out_specs=pl.BlockSpec((tm, D_MODEL), row),
        out_shape=jax.ShapeDtypeStruct((m, D_MODEL), F32),
        compiler_params=_cparams("arbitrary"),
        name="out_proj_ln",
    )(y_all, x, w, g, b)


def _swiglu_kernel(te_ref, tv_ref, x_ref, wg_ref, wu_ref, wd_ref, g_ref, b_ref, o_ref, acc_ref, xb_ref, *, fuse_ln):
    i = pl.program_id(0)
    f = pl.program_id(1)
    nf = pl.num_programs(1)
    live = tv_ref[i] > 0

    @pl.when(live & (f == 0))
    def _():
        xb_ref[...] = x_ref[...].reshape(xb_ref.shape).astype(BF16)

    @pl.when(live)
    def _():
        xb = xb_ref[...]
        hg = jnp.dot(xb, wg_ref[0], preferred_element_type=F32)
        hu = jnp.dot(xb, wu_ref[0], preferred_element_type=F32)
        part = jnp.dot((_silu(hg) * hu).astype(BF16), wd_ref[0], preferred_element_type=F32)

        @pl.when(f == 0)
        def _():
            acc_ref[...] = part

        @pl.when(f > 0)
        def _():
            acc_ref[...] += part

    @pl.when(live & (f == nf - 1))
    def _():
        if fuse_ln:
            o_ref[...] = _layer_norm(ALPHA * x_ref[...] + acc_ref[...], g_ref[...], b_ref[...])
        else:
            o_ref[...] = acc_ref[...].reshape(o_ref.shape)

    @pl.when(jnp.logical_not(live) & (f == nf - 1))
    def _():
        o_ref[...] = jnp.zeros_like(o_ref)


def _swiglu(x, tile_expert, tile_live, wg, wu, wd, g, b, *, tm, fuse_ln):
    m = x.shape[0]
    assert m % tm == 0 and tile_expert.shape[0] == m // tm and tile_live.shape[0] == m // tm
    row_block = (tm,) + x.shape[1:]
    row_map = (lambda i, f, te, tv: (i, 0)) if x.ndim == 2 else (lambda i, f, te, tv: (i, 0, 0))
    d_ff = wg.shape[2]
    ff = FF_TILE if d_ff % FF_TILE == 0 else d_ff // 2
    assert d_ff % ff == 0 and ff % LANES == 0
    nf = d_ff // ff
    fsel = lambda f, tv_i: f * tv_i + (nf - 1) * (1 - tv_i)
    fix = lambda i, f, te, tv: (0, 0)
    grid_spec = pltpu.PrefetchScalarGridSpec(
        num_scalar_prefetch=2,
        grid=(m // tm, nf),
        in_specs=[pl.BlockSpec(row_block, row_map),
                  pl.BlockSpec((1, D_MODEL, ff), lambda i, f, te, tv: (te[i], 0, fsel(f, tv[i]))),
                  pl.BlockSpec((1, D_MODEL, ff), lambda i, f, te, tv: (te[i], 0, fsel(f, tv[i]))),
                  pl.BlockSpec((1, ff, D_MODEL), lambda i, f, te, tv: (te[i], fsel(f, tv[i]), 0)),
                  pl.BlockSpec((1, D_MODEL), fix), pl.BlockSpec((1, D_MODEL), fix)],
        out_specs=pl.BlockSpec(row_block, row_map),
        scratch_shapes=[pltpu.VMEM((tm, D_MODEL), F32), pltpu.VMEM((tm, D_MODEL), BF16)],
    )
    return pl.pallas_call(
        functools.partial(_swiglu_kernel, fuse_ln=fuse_ln),
        grid_spec=grid_spec,
        out_shape=jax.ShapeDtypeStruct(x.shape, F32),
        compiler_params=_cparams("arbitrary", "arbitrary"),
        name="swiglu_ln" if fuse_ln else "swiglu_grouped",
    )(tile_expert, tile_live, x, wg, wu, wd, g, b)


def _router_kernel(x_ref, wr_ref, idx_ref, gate_ref, xt_ref):
    x = x_ref[...]
    xt_ref[...] = x.reshape(xt_ref.shape)
    logits = jnp.dot(x.astype(BF16), wr_ref[...], preferred_element_type=F32)
    lane = lax.broadcasted_iota(jnp.int32, logits.shape, 1)
    logits = jnp.where(lane < N_EXPERTS, logits, -jnp.inf)
    v1 = jnp.max(logits, axis=-1, keepdims=True)
    i1 = jnp.min(jnp.where(logits == v1, lane, LANES), axis=-1, keepdims=True)
    rest = jnp.where(lane == i1, -jnp.inf, logits)
    v2 = jnp.max(rest, axis=-1, keepdims=True)
    i2 = jnp.min(jnp.where(rest == v2, lane, LANES), axis=-1, keepdims=True)
    e2 = jnp.exp(v2 - v1)
    den = 1.0 + e2
    idx_ref[...] = jnp.where(lane == 0, i1, jnp.where(lane == 1, i2, 0))
    gate_ref[...] = jnp.where(lane == 0, 1.0 / den, jnp.where(lane == 1, e2 / den, 0.0))


def _router(x, wr):
    m = x.shape[0]
    tm = min(TOKEN_TILE, m)
    row = lambda i: (i, 0)
    return pl.pallas_call(
        _router_kernel,
        grid=(m // tm,),
        in_specs=[pl.BlockSpec((tm, D_MODEL), row), pl.BlockSpec((D_MODEL, LANES), lambda i: (0, 0))],
        out_specs=[pl.BlockSpec((tm, LANES), row), pl.BlockSpec((tm, LANES), row),
                   pl.BlockSpec((tm,) + TOKEN_TILE_SHAPE, lambda i: (i, 0, 0))],
        out_shape=[jax.ShapeDtypeStruct((m, LANES), jnp.int32), jax.ShapeDtypeStruct((m, LANES), F32),
                   jax.ShapeDtypeStruct((m,) + TOKEN_TILE_SHAPE, F32)],
        compiler_params=_cparams("arbitrary"),
        name="router_top2",
    )(x, wr)


def _row_gather_kernel(idx_ref, src_ref, dst_ref, sem, *, rows):
    base = pl.program_id(0) * rows

    def start(r, carry):
        pltpu.make_async_copy(src_ref.at[pl.ds(idx_ref[0, 0, r], 1)], dst_ref.at[pl.ds(base + r, 1)], sem).start()
        return carry

    lax.fori_loop(0, rows, start, 0, unroll=8)
    pltpu.make_async_copy(src_ref.at[pl.ds(0, rows)], dst_ref.at[pl.ds(base, rows)], sem).wait()


def _row_gather(src, idx, rows):
    n_rows = idx.shape[0]
    assert n_rows % rows == 0
    steps = n_rows // rows
    return pl.pallas_call(
        functools.partial(_row_gather_kernel, rows=rows),
        grid=(steps,),
        in_specs=[pl.BlockSpec((1, 1, rows), lambda i: (i, 0, 0), memory_space=pltpu.SMEM),
                  pl.BlockSpec(memory_space=pl.ANY)],
        out_specs=pl.BlockSpec(memory_space=pl.ANY),
        out_shape=jax.ShapeDtypeStruct((n_rows,) + src.shape[1:], src.dtype),
        scratch_shapes=[pltpu.SemaphoreType.DMA],
        compiler_params=_cparams("arbitrary"),
        name="row_gather",
    )(idx.reshape(steps, 1, rows), src)


def _combine_ln_kernel(x_ref, y_ref, gate_ref, g_ref, b_ref, o_ref):
    gate = gate_ref[...]
    x = x_ref[...]
    f = gate[:, 0:1] * y_ref[:, 0].reshape(x.shape) + gate[:, 1:2] * y_ref[:, 1].reshape(x.shape)
    o_ref[...] = _layer_norm(ALPHA * x + f, g_ref[...], b_ref[...])


def _combine_ln(x, y_pairs, gates, g, b):
    m = x.shape[0]
    tm = min(TOKEN_TILE, m)
    row = lambda i: (i, 0)
    fix = lambda i: (0, 0)
    return pl.pallas_call(
        _combine_ln_kernel,
        grid=(m // tm,),
        in_specs=[pl.BlockSpec((tm, D_MODEL), row),
                  pl.BlockSpec((tm, TOP_K) + TOKEN_TILE_SHAPE, lambda i: (i, 0, 0, 0)),
                  pl.BlockSpec((tm, LANES), row),
                  pl.BlockSpec((1, D_MODEL), fix), pl.BlockSpec((1, D_MODEL), fix)],
        out_specs=pl.BlockSpec((tm, D_MODEL), row),
        out_shape=jax.ShapeDtypeStruct((m, D_MODEL), F32),
        compiler_params=_cparams("arbitrary"),
        name="moe_combine_ln",
    )(x, y_pairs, gates, g, b)


def _moe_ffn_ln(x, wr, wg, wu, wd, g, b):
    m = x.shape[0]
    tm = min(TOKEN_TILE, m)
    top_i, gates, x_tiles = _router(x, wr)
    e_flat = top_i[:, :TOP_K].reshape(-1)
    onehot = (e_flat[:, None] == jnp.arange(N_EXPERTS, dtype=jnp.int32)[None, :]).astype(jnp.int32)
    rank = jnp.take_along_axis(jnp.cumsum(onehot, axis=0), e_flat[:, None], axis=1)[:, 0] - 1
    counts = jnp.sum(onehot, axis=0)
    padded = ((counts + tm - 1) // tm) * tm
    ends = jnp.cumsum(padded)
    starts = ends - padded
    pos = starts[e_flat] + rank
    n_pad = TOP_K * m + N_EXPERTS * tm
    n_tiles = n_pad // tm
    row_token = jnp.zeros((n_pad,), jnp.int32).at[pos].set(jnp.arange(TOP_K * m, dtype=jnp.int32) // TOP_K)
    tile_start = jnp.arange(n_tiles, dtype=jnp.int32) * tm
    tile_expert = jnp.minimum(jnp.sum((tile_start[:, None] >= ends[None, :]).astype(jnp.int32), axis=1),
                              N_EXPERTS - 1).astype(jnp.int32)
    tile_live = (tile_start < ends[N_EXPERTS - 1]).astype(jnp.int32)
    xs = _row_gather(x_tiles, row_token, tm)
    ys = _swiglu(xs, tile_expert, tile_live, wg, wu, wd, g, b, tm=tm, fuse_ln=False)
    y_pairs = _row_gather(ys, pos.astype(jnp.int32), tm).reshape((m, TOP_K) + TOKEN_TILE_SHAPE)
    return _combine_ln(x, y_pairs, gates, g, b)


def _shift_rows(x, tail, sh):
    if sh == 0:
        return x
    row8 = lax.broadcasted_iota(jnp.int32, (SUBLANES, 1), 0)
    xr = pltpu.roll(x, sh, 0)
    head = jnp.where(row8 < sh, pltpu.roll(tail, sh, 0), xr[0:SUBLANES])
    return jnp.concatenate([head, xr[SUBLANES:]], axis=0)


def _causal_conv(x, tail, w_ref, b_ref):
    acc = b_ref[...] + _shift_rows(x, tail, CONV_WIDTH - 1) * w_ref[0:1, :]
    for j in range(1, CONV_WIDTH):
        acc = acc + _shift_rows(x, tail, CONV_WIDTH - 1 - j) * w_ref[j:j + 1, :]
    return acc


def _lru_gates(xl, wax_ref, bax_ref, lam_ref):
    gates = jnp.dot(xl.astype(BF16), wax_ref[...], preferred_element_type=F32) + bax_ref[...]
    r = jax.nn.sigmoid(gates[:, :LRU_WIDTH])
    i = jax.nn.sigmoid(gates[:, LRU_WIDTH:])
    log_a = -LRU_C * r * jax.nn.softplus(-lam_ref[...])
    a = jnp.exp(log_a)
    th = jnp.tanh(log_a)
    u = jnp.sqrt(-2.0 * th / (1.0 - th)) * (i * xl)
    return a, u


def _sink_softmax(s, sink):
    m = jnp.maximum(jnp.max(s, axis=-1, keepdims=True), sink)
    e = jnp.exp(s - m)
    den = jnp.sum(e, axis=-1, keepdims=True) + jnp.exp(sink - m)
    return e, den


def _mixer_prompt_kernel(sink_ref, proj_ref, cw_ref, cb_ref, dtb_ref, alog_ref, drow_ref, ng_ref, rexp_ref,
                         gmask_ref, lcw_ref, lcb_ref, wax_ref, bax_ref, lam_ref,
                         y_ref, s_out_ref, hl_out_ref,
                         xtail, ltail, s_scr, hl_scr, kprev, vprev):
    c = pl.program_id(1)
    nc = pl.num_programs(1)
    t = CHUNK

    @pl.when(c == 0)
    def _():
        xtail[...] = jnp.zeros_like(xtail)
        ltail[...] = jnp.zeros_like(ltail)
        s_scr[...] = jnp.zeros_like(s_scr)
        hl_scr[...] = jnp.zeros_like(hl_scr)
        kprev[...] = jnp.zeros_like(kprev)
        vprev[...] = jnp.zeros_like(vprev)

    row = lax.broadcasted_iota(jnp.int32, (t, 1), 0)
    col = lax.broadcasted_iota(jnp.int32, (1, t), 1)
    lane = lax.broadcasted_iota(jnp.int32, (1, LANES), 1)
    causal = row >= col

    z = proj_ref[:, OFF_Z:OFF_Z + SSD_WIDTH]
    xbc_raw = proj_ref[:, OFF_XBC:OFF_XBC + SSD_CONV_DIM]
    xbc = _silu(_causal_conv(xbc_raw, xtail[...], cw_ref, cb_ref))
    xtail[...] = xbc_raw[t - SUBLANES:t]
    xs = xbc[:, :SSD_WIDTH]
    bm = xbc[:, SSD_WIDTH:SSD_WIDTH + SSD_GN]
    cm = xbc[:, SSD_WIDTH + SSD_GN:]
    dt = jax.nn.softplus(proj_ref[:, OFF_DT:OFF_DT + LANES] + dtb_ref[...])
    da = dt * (-jnp.exp(alog_ref[...]))
    acum = _select_dot_lhs(causal.astype(BF16), da)
    acum_t = acum.T
    rexp = rexp_ref[...]
    acum_e = _select_dot_rhs(acum, rexp)
    dt_e = _select_dot_rhs(dt, rexp)
    xdt = xs * dt_e
    exp_e = jnp.exp(acum_e)
    last_e = acum_e[t - 1:t, :]
    xw = xdt * jnp.exp(last_e - acum_e)
    bmb = bm.astype(BF16)

    pairs = []
    for g in range(SSD_GROUPS):
        gsel = (lane >= g * SSD_STATE) & (lane < (g + 1) * SSD_STATE)
        cb = _dot_nt(jnp.where(gsel, cm, 0.0), bmb)
        for pc in range(SSD_HPG // 2):
            p2 = g * (SSD_HPG // 2) + pc
            ms = []
            for j in (2 * p2, 2 * p2 + 1):
                seg = acum[:, j:j + 1] - acum_t[j:j + 1, :]
                dec = jnp.exp(jnp.where(causal, seg, -jnp.inf))
                ms.append((cb * dec).astype(BF16))
            lhs = jnp.concatenate(ms, axis=1)
            xp = xdt[:, LANES * p2:LANES * (p2 + 1)]
            rhs = jnp.concatenate([jnp.where(lane < SSD_HEAD_DIM, xp, 0.0),
                                   jnp.where(lane >= SSD_HEAD_DIM, xp, 0.0)], axis=0).astype(BF16)
            pairs.append(jnp.dot(lhs, rhs, preferred_element_type=F32))
    y_diag = jnp.concatenate(pairs, axis=1)

    s_old = s_scr[...]
    y_off = _dot(cm, s_old) * exp_e
    btx = _dot(bm.T, xw)
    s_scr[...] = s_old * exp_e[t - 1:t, :] + jnp.where(gmask_ref[...] > 0.0, btx, 0.0)

    y = (y_diag + y_off + drow_ref[...] * xs) * _silu(z)
    half = SSD_WIDTH // SSD_GROUPS
    normed = []
    for g in range(SSD_GROUPS):
        seg = y[:, g * half:(g + 1) * half]
        normed.append(seg * lax.rsqrt(jnp.mean(seg * seg, axis=-1, keepdims=True) + RMS_EPS))
    y_ssd = jnp.concatenate(normed, axis=1) * ng_ref[...]

    lx_raw = proj_ref[:, OFF_LX:OFF_LX + LRU_WIDTH]
    xl = _causal_conv(lx_raw, ltail[...], lcw_ref, lcb_ref)
    ltail[...] = lx_raw[t - SUBLANES:t]
    a_s, u_s = _lru_gates(xl, wax_ref, bax_ref, lam_ref)
    d = 1
    while d < t:
        a_sh = jnp.where(row >= d, pltpu.roll(a_s, d, 0), 1.0)
        u_sh = jnp.where(row >= d, pltpu.roll(u_s, d, 0), 0.0)
        u_s = a_s * u_sh + u_s
        a_s = a_s * a_sh
        d *= 2
    h_seq = a_s * hl_scr[...] + u_s
    hl_scr[...] = h_seq[t - 1:t, :]
    y_lru = h_seq * jax.nn.gelu(proj_ref[:, OFF_LGATE:OFF_LGATE + LRU_WIDTH])

    k = proj_ref[:, OFF_K:OFF_K + KV_DIM]
    v = proj_ref[:, OFF_V:OFF_V + KV_DIM]
    kc = jnp.concatenate([kprev[...], k], axis=0).astype(BF16)
    vc = jnp.concatenate([vprev[...], v], axis=0).astype(BF16)
    kprev[...] = k
    vprev[...] = v
    si = lax.broadcasted_iota(jnp.int32, (1, 2 * t), 1)
    first_key = jnp.where(c > 0, 0, t)
    valid = (si > row) & (si <= row + t) & (si >= first_key)
    blocks = []
    for g in range(ATTN_GROUP):
        qb = proj_ref[:, OFF_Q + LANES * g:OFF_Q + LANES * (g + 1)]
        halves = []
        for kv in range(ATTN_KV_HEADS):
            hsel = (lane >= kv * ATTN_HEAD_DIM) & (lane < (kv + 1) * ATTN_HEAD_DIM)
            s = _dot_nt(jnp.where(hsel, qb, 0.0), kc) * ATTN_SCALE
            s = jnp.where(valid, s, -jnp.inf)
            e, den = _sink_softmax(s, sink_ref[kv * ATTN_GROUP + g])
            halves.append(jnp.dot((e / den).astype(BF16), vc, preferred_element_type=F32))
        blocks.append(jnp.where(lane < ATTN_HEAD_DIM, halves[0], halves[1]))
    y_attn = jnp.concatenate(blocks, axis=1)

    y_ref[...] = jnp.concatenate([y_ssd, y_lru, y_attn], axis=1).astype(y_ref.dtype)

    @pl.when(c == nc - 1)
    def _():
        s_out_ref[0] = s_scr[...]
        hl_out_ref[0] = hl_scr[...]


def _mixer_prompt(proj, lw, bsz, seq):
    nc = seq // CHUNK
    fix = lambda b, c: (0, 0)
    vec = lambda n: pl.BlockSpec((1, n), fix)
    in_specs = [
        pl.BlockSpec(memory_space=pltpu.SMEM),
        pl.BlockSpec((CHUNK, PROJ_WIDTH), lambda b, c: (b * nc + c, 0)),
        pl.BlockSpec((CONV_WIDTH, SSD_CONV_DIM), fix), vec(SSD_CONV_DIM),
        vec(LANES), vec(LANES), vec(SSD_WIDTH), vec(SSD_WIDTH),
        pl.BlockSpec((LANES, SSD_WIDTH), fix), pl.BlockSpec((SSD_GN, SSD_WIDTH), fix),
        pl.BlockSpec((CONV_WIDTH, LRU_WIDTH), fix), vec(LRU_WIDTH),
        pl.BlockSpec((LRU_WIDTH, 2 * LRU_WIDTH), fix), vec(2 * LRU_WIDTH), vec(LRU_WIDTH),
    ]
    out_specs = [
        pl.BlockSpec((CHUNK, MIX_WIDTH), lambda b, c: (b * nc + c, 0)),
        pl.BlockSpec((1, SSD_GN, SSD_WIDTH), lambda b, c: (b, 0, 0)),
        pl.BlockSpec((1, 1, LRU_WIDTH), lambda b, c: (b, 0, 0)),
    ]
    out_shape = [
        jax.ShapeDtypeStruct((bsz * seq, MIX_WIDTH), BF16),
        jax.ShapeDtypeStruct((bsz, SSD_GN, SSD_WIDTH), F32),
        jax.ShapeDtypeStruct((bsz, 1, LRU_WIDTH), F32),
    ]
    scratch = [
        pltpu.VMEM((SUBLANES, SSD_CONV_DIM), F32), pltpu.VMEM((SUBLANES, LRU_WIDTH), F32),
        pltpu.VMEM((SSD_GN, SSD_WIDTH), F32), pltpu.VMEM((1, LRU_WIDTH), F32),
        pltpu.VMEM((CHUNK, KV_DIM), F32), pltpu.VMEM((CHUNK, KV_DIM), F32),
    ]
    return pl.pallas_call(
        _mixer_prompt_kernel,
        grid=(bsz, nc),
        in_specs=in_specs,
        out_specs=out_specs,
        out_shape=out_shape,
        scratch_shapes=scratch,
        compiler_params=_cparams("arbitrary", "arbitrary"),
        name="mixer_prompt",
    )(lw['sinks'], proj, lw['ssd_conv_w'], lw['ssd_conv_b'], lw['dt_bias'], lw['a_log'], lw['d_row'],
      lw['norm_g'], lw['head_expand'], lw['group_mask'], lw['lru_conv_w'], lw['lru_conv_b'],
      lw['lru_wax'], lw['lru_bax'], lw['lru_lambda'])


DEC_ROWS = SUBLANES


def _conv_step(x, buf_ref, w_ref, b_ref):
    acc = b_ref[...] + buf_ref[0] * w_ref[0:1, :]
    for j in range(1, CONV_WIDTH - 1):
        acc = acc + buf_ref[j] * w_ref[j:j + 1, :]
    return acc + x * w_ref[CONV_WIDTH - 1:CONV_WIDTH, :]


def _mixer_decode_kernel(sink_ref, proj_ref, sbuf_ref, lbuf_ref, hl_ref, kc_ref, vc_ref, h_ref,
                         cw_ref, cb_ref, dtb_ref, alog_ref, drow_ref, ng_ref, rexp_ref,
                         lcw_ref, lcb_ref, wax_ref, bax_ref, lam_ref,
                         y_ref, sbuf_out, lbuf_out, hl_out, kc_out, vc_out, h_out):
    nb = DEC_ROWS
    row8 = lax.broadcasted_iota(jnp.int32, (nb, 1), 0)
    lane = lax.broadcasted_iota(jnp.int32, (1, LANES), 1)
    wrow = lax.broadcasted_iota(jnp.int32, (WINDOW, 1), 0)

    z = proj_ref[:, OFF_Z:OFF_Z + SSD_WIDTH]
    xbc_raw = proj_ref[:, OFF_XBC:OFF_XBC + SSD_CONV_DIM]
    xbc = _silu(_conv_step(xbc_raw, sbuf_ref, cw_ref, cb_ref))
    for j in range(CONV_WIDTH - 2):
        sbuf_out[j] = sbuf_ref[j + 1]
    sbuf_out[CONV_WIDTH - 2] = xbc_raw
    xs = xbc[:, :SSD_WIDTH]
    bm = xbc[:, SSD_WIDTH:SSD_WIDTH + SSD_GN]
    cm = xbc[:, SSD_WIDTH + SSD_GN:]
    dt = jax.nn.softplus(proj_ref[:, OFF_DT:OFF_DT + LANES] + dtb_ref[...])
    da = dt * (-jnp.exp(alog_ref[...]))
    rexp = rexp_ref[...]
    decay_e = jnp.exp(_select_dot_rhs(da, rexp))
    xdt = xs * _select_dot_rhs(dt, rexp)
    cols = jnp.concatenate([xdt, decay_e, jnp.zeros((LANES - 2 * nb, SSD_WIDTH), F32)], axis=0)
    cols_t = [cols[:, LANES * p2:LANES * (p2 + 1)].T for p2 in range(SSD_HEADS // 2)]
    bm_g = [bm[:, :SSD_STATE], pltpu.roll(bm, SSD_STATE, 1)[:, :SSD_STATE]]
    cm_g = [cm[:, :SSD_STATE], pltpu.roll(cm, SSD_STATE, 1)[:, :SSD_STATE]]
    half = SSD_WIDTH // SSD_GROUPS
    y_off = [jnp.zeros((nb, half), F32) for _ in range(SSD_GROUPS)]
    for bb in range(nb):
        for g in range(SSD_GROUPS):
            h_g = h_ref[bb, g * SSD_HPG:(g + 1) * SSD_HPG].reshape(half, SSD_STATE)
            y_off[g] = jnp.where(row8 == bb, _dot_nt(cm_g[g], h_g), y_off[g])
        for p2 in range(SSD_HEADS // 2):
            g = p2 // (SSD_HPG // 2)
            h_pair = h_ref[bb, 2 * p2:2 * p2 + 2].reshape(2 * SSD_HEAD_DIM, SSD_STATE)
            xcol = cols_t[p2][:, bb:bb + 1]
            dcol = cols_t[p2][:, nb + bb:nb + bb + 1]
            h_new = h_pair * dcol + xcol * bm_g[g][bb:bb + 1, :]
            h_out[bb, 2 * p2:2 * p2 + 2] = h_new.reshape(2, SSD_HEAD_DIM, SSD_STATE)
    cbv = cm * bm
    ys = []
    for g in range(SSD_GROUPS):
        gsel = (lane >= g * SSD_STATE) & (lane < (g + 1) * SSD_STATE)
        cb_g = jnp.sum(jnp.where(gsel, cbv, 0.0), axis=-1, keepdims=True)
        sl = slice(g * half, (g + 1) * half)
        ys.append(cb_g * xdt[:, sl] + y_off[g] * decay_e[:, sl])
    y = (jnp.concatenate(ys, axis=1) + drow_ref[...] * xs) * _silu(z)
    normed = []
    for g in range(SSD_GROUPS):
        seg = y[:, g * half:(g + 1) * half]
        normed.append(seg * lax.rsqrt(jnp.mean(seg * seg, axis=-1, keepdims=True) + RMS_EPS))
    y_ssd = jnp.concatenate(normed, axis=1) * ng_ref[...]

    lx_raw = proj_ref[:, OFF_LX:OFF_LX + LRU_WIDTH]
    xl = _conv_step(lx_raw, lbuf_ref, lcw_ref, lcb_ref)
    for j in range(CONV_WIDTH - 2):
        lbuf_out[j] = lbuf_ref[j + 1]
    lbuf_out[CONV_WIDTH - 2] = lx_raw
    a_s, u_s = _lru_gates(xl, wax_ref, bax_ref, lam_ref)
    h_lru = a_s * hl_ref[...] + u_s
    hl_out[...] = h_lru
    y_lru = h_lru * jax.nn.gelu(proj_ref[:, OFF_LGATE:OFF_LGATE + LRU_WIDTH])

    k_new = proj_ref[:, OFF_K:OFF_K + KV_DIM]
    v_new = proj_ref[:, OFF_V:OFF_V + KV_DIM]
    head_row_g = row8 // ATTN_KV_HEADS
    head_row_kv = row8 % ATTN_KV_HEADS
    lane_kv = lane // ATTN_HEAD_DIM
    sink_col = jnp.zeros((nb, 1), F32)
    for r in range(ATTN_Q_HEADS):
        sink_col = jnp.where(row8 == r, sink_ref[(r % ATTN_KV_HEADS) * ATTN_GROUP + r // ATTN_KV_HEADS], sink_col)
    y_blocks = [jnp.zeros((nb, LANES), F32) for _ in range(ATTN_GROUP)]
    for bb in range(nb):
        q_rows = jnp.zeros((ATTN_Q_HEADS, LANES), F32)
        for g in range(ATTN_GROUP):
            qb = proj_ref[bb:bb + 1, OFF_Q + LANES * g:OFF_Q + LANES * (g + 1)]
            q_rows = jnp.where((head_row_g == g) & (head_row_kv == lane_kv), qb, q_rows)
        kb = kc_ref[bb]
        vb = vc_ref[bb]
        kn = k_new[bb:bb + 1, :]
        vn = v_new[bb:bb + 1, :]
        s = _dot_nt(q_rows, kb) * ATTN_SCALE
        s = jnp.where(lane >= 1, s, -jnp.inf)
        s_new = jnp.sum(q_rows * kn, axis=-1, keepdims=True) * ATTN_SCALE
        m = jnp.maximum(jnp.maximum(jnp.max(s, axis=-1, keepdims=True), s_new), sink_col)
        e = jnp.exp(s - m)
        e_new = jnp.exp(s_new - m)
        den = jnp.sum(e, axis=-1, keepdims=True) + e_new + jnp.exp(sink_col - m)
        o = _dot(e / den, vb) + (e_new / den) * vn
        for g in range(ATTN_GROUP):
            blk = jnp.where(lane < ATTN_HEAD_DIM, o[2 * g:2 * g + 1, :], o[2 * g + 1:2 * g + 2, :])
            y_blocks[g] = jnp.where(row8 == bb, blk, y_blocks[g])
        kc_out[bb] = jnp.where(wrow == WINDOW - 1, kn, pltpu.roll(kb, WINDOW - 1, 0))
        vc_out[bb] = jnp.where(wrow == WINDOW - 1, vn, pltpu.roll(vb, WINDOW - 1, 0))
    y_attn = jnp.concatenate(y_blocks, axis=1)

    y_ref[...] = jnp.concatenate([y_ssd, y_lru, y_attn], axis=1)


def _mixer_decode(proj, lw, sbuf, lbuf, hl, kc, vc, h):
    bsz = proj.shape[0]
    nb = DEC_ROWS
    fix = lambda i: (0, 0)
    vec = lambda n: pl.BlockSpec((1, n), fix)
    rows = lambda n: pl.BlockSpec((nb, n), lambda i: (i, 0))
    buf = lambda n: pl.BlockSpec((CONV_WIDTH - 1, nb, n), lambda i: (0, i, 0))
    cache = pl.BlockSpec((nb, WINDOW, KV_DIM), lambda i: (i, 0, 0))
    state = pl.BlockSpec((nb, SSD_HEADS, SSD_HEAD_DIM, SSD_STATE), lambda i: (i, 0, 0, 0))
    in_specs = [
        pl.BlockSpec(memory_space=pltpu.SMEM),
        rows(PROJ_WIDTH), buf(SSD_CONV_DIM), buf(LRU_WIDTH), rows(LRU_WIDTH), cache, cache, state,
        pl.BlockSpec((CONV_WIDTH, SSD_CONV_DIM), fix), vec(SSD_CONV_DIM),
        vec(LANES), vec(LANES), vec(SSD_WIDTH), vec(SSD_WIDTH),
        pl.BlockSpec((LANES, SSD_WIDTH), fix),
        pl.BlockSpec((CONV_WIDTH, LRU_WIDTH), fix), vec(LRU_WIDTH),
        pl.BlockSpec((LRU_WIDTH, 2 * LRU_WIDTH), fix), vec(2 * LRU_WIDTH), vec(LRU_WIDTH),
    ]
    out_specs = [rows(MIX_WIDTH), buf(SSD_CONV_DIM), buf(LRU_WIDTH), rows(LRU_WIDTH), cache, cache, state]
    out_shape = [
        jax.ShapeDtypeStruct((bsz, MIX_WIDTH), F32),
        jax.ShapeDtypeStruct(sbuf.shape, F32), jax.ShapeDtypeStruct(lbuf.shape, F32),
        jax.ShapeDtypeStruct(hl.shape, F32), jax.ShapeDtypeStruct(kc.shape, F32),
        jax.ShapeDtypeStruct(vc.shape, F32), jax.ShapeDtypeStruct(h.shape, F32),
    ]
    return pl.pallas_call(
        _mixer_decode_kernel,
        grid=(bsz // nb,),
        in_specs=in_specs,
        out_specs=out_specs,
        out_shape=out_shape,
        compiler_params=_cparams("arbitrary"),
        name="mixer_decode",
    )(lw['sinks'], proj, sbuf, lbuf, hl, kc, vc, h,
      lw['ssd_conv_w'], lw['ssd_conv_b'], lw['dt_bias'], lw['a_log'], lw['d_row'], lw['norm_g'],
      lw['head_expand'], lw['lru_conv_w'], lw['lru_conv_b'], lw['lru_wax'], lw['lru_bax'], lw['lru_lambda'])


def _q_perm():
    g, kv, d = jnp.meshgrid(jnp.arange(ATTN_GROUP), jnp.arange(ATTN_KV_HEADS), jnp.arange(ATTN_HEAD_DIM),
                            indexing='ij')
    return (kv * ATTN_GROUP * ATTN_HEAD_DIM + g * ATTN_HEAD_DIM + d).reshape(-1)


def _block_diag(w):
    eye = jnp.eye(LRU_BLOCKS, dtype=w.dtype)
    return jnp.einsum('kij,kl->kilj', w, eye).reshape(LRU_WIDTH, LRU_WIDTH)


def _prep_layer(l, p):
    w_in = p['w_in'][l]
    o = 0
    segs = {}
    for name, size in (('z', SSD_WIDTH), ('xbc', SSD_CONV_DIM), ('dt', SSD_HEADS), ('lgate', LRU_WIDTH),
                       ('lx', LRU_WIDTH), ('q', ATTN_WIDTH), ('k', KV_DIM), ('v', KV_DIM)):
        segs[name] = w_in[:, o:o + size]
        o += size
    qp = _q_perm()
    w_in_p = jnp.concatenate(
        [segs['z'], segs['xbc'], segs['lgate'], segs['lx'], segs['q'][:, qp], segs['k'], segs['v'],
         jnp.pad(segs['dt'], ((0, 0), (0, LANES - SSD_HEADS)))], axis=1).astype(BF16)
    w_out = p['w_out'][l]
    attn_rows = w_out[SSD_WIDTH + LRU_WIDTH:]
    w_out_p = jnp.concatenate([w_out[:SSD_WIDTH + LRU_WIDTH], attn_rows[qp]], axis=0).astype(BF16)
    pad_h = lambda v: jnp.pad(v, (0, LANES - SSD_HEADS)).reshape(1, LANES)
    head_of_lane = jnp.arange(SSD_WIDTH) // SSD_HEAD_DIM
    head_expand = (jnp.arange(LANES)[:, None] == head_of_lane[None, :]).astype(BF16)
    group_mask = ((jnp.arange(SSD_GN)[:, None] // SSD_STATE) == (head_of_lane[None, :] // SSD_HPG)).astype(F32)
    return dict(
        w_in=w_in_p, w_out=w_out_p,
        ssd_conv_w=p['ssd_conv_w'][l], ssd_conv_b=p['ssd_conv_b'][l].reshape(1, -1),
        dt_bias=pad_h(p['ssd_dt_bias'][l]), a_log=pad_h(p['ssd_a_log'][l]),
        d_row=jnp.repeat(p['ssd_d'][l], SSD_HEAD_DIM).reshape(1, -1),
        norm_g=p['ssd_norm_g'][l].reshape(1, -1),
        head_expand=head_expand, group_mask=group_mask,
        lru_conv_w=p['lru_conv_w'][l], lru_conv_b=p['lru_conv_b'][l].reshape(1, -1),
        lru_wax=jnp.concatenate([_block_diag(p['lru_wa'][l]), _block_diag(p['lru_wx'][l])], axis=1).astype(BF16),
        lru_bax=jnp.concatenate([p['lru_ba'][l].reshape(-1), p['lru_bx'][l].reshape(-1)]).reshape(1, -1),
        lru_lambda=p['lru_lambda'][l].reshape(1, -1),
        sinks=p['attn_sinks'][l],
        channel=_prep_channel(l, p),
        ln1_g=p['ln1_g'][l].reshape(1, -1), ln1_b=p['ln1_b'][l].reshape(1, -1),
        ln2_g=p['ln2_g'][l].reshape(1, -1), ln2_b=p['ln2_b'][l].reshape(1, -1),
    )


def _prep_channel(l, p):
    i = l // 2
    if l % 2 == 0:
        return dict(wg=p['ffn_w_gate'][i][None].astype(BF16), wu=p['ffn_w_up'][i][None].astype(BF16),
                    wd=p['ffn_w_down'][i][None].astype(BF16))
    return dict(wr=jnp.pad(p['moe_router'][i], ((0, 0), (0, LANES - N_EXPERTS))).astype(BF16),
                wg=p['moe_w_gate'][i].astype(BF16), wu=p['moe_w_up'][i].astype(BF16),
                wd=p['moe_w_down'][i].astype(BF16))


def _channel_mixer(l, x, lw):
    cw = lw['channel']
    if l % 2 == 0:
        tm = min(TOKEN_TILE, x.shape[0])
        n_tiles = x.shape[0] // tm
        return _swiglu(x, jnp.zeros((n_tiles,), jnp.int32), jnp.ones((n_tiles,), jnp.int32),
                       cw['wg'], cw['wu'], cw['wd'], lw['ln2_g'], lw['ln2_b'], tm=tm, fuse_ln=True)
    return _moe_ffn_ln(x, cw['wr'], cw['wg'], cw['wu'], cw['wd'], lw['ln2_g'], lw['ln2_b'])


def _ssd_state_from_scratch_layout(s):
    bsz = s.shape[0]
    s6 = s.reshape(bsz, SSD_GROUPS, SSD_STATE, SSD_GROUPS, SSD_HPG, SSD_HEAD_DIM)
    diag = jnp.stack([s6[:, g, :, g] for g in range(SSD_GROUPS)], axis=1)
    return jnp.transpose(diag, (0, 1, 3, 4, 2)).reshape(bsz, SSD_HEADS, SSD_HEAD_DIM, SSD_STATE)


def _prompt_trunk(x_prompt, lws, p):
    bsz, seq, _ = x_prompt.shape
    x = x_prompt.reshape(bsz * seq, D_MODEL)
    states = [[] for _ in range(6)]
    for l in range(DEPTH):
        lw = lws[l]
        proj = _in_proj(x, lw['w_in'])
        y_all, s_fin, hl_fin = _mixer_prompt(proj, lw, bsz, seq)
        proj3 = proj.reshape(bsz, seq, PROJ_WIDTH)
        states[0].append(_ssd_state_from_scratch_layout(s_fin))
        states[1].append(proj3[:, seq - (CONV_WIDTH - 1):, OFF_XBC:OFF_XBC + SSD_CONV_DIM])
        states[2].append(hl_fin.reshape(bsz, LRU_WIDTH))
        states[3].append(proj3[:, seq - (CONV_WIDTH - 1):, OFF_LX:OFF_LX + LRU_WIDTH])
        states[4].append(proj3[:, seq - WINDOW:, OFF_K:OFF_K + KV_DIM].reshape(bsz, WINDOW, ATTN_KV_HEADS, ATTN_HEAD_DIM))
        states[5].append(proj3[:, seq - WINDOW:, OFF_V:OFF_V + KV_DIM].reshape(bsz, WINDOW, ATTN_KV_HEADS, ATTN_HEAD_DIM))
        x = _out_proj_ln(y_all, x, lw['w_out'], lw['ln1_g'], lw['ln1_b'])
        x = _channel_mixer(l, x, lw)
    return x.reshape(bsz, seq, D_MODEL), [jnp.stack(s, axis=0) for s in states]


def _sample_trunk(x_sample, lws, p, state_ssd, state_ssd_conv, state_lru, state_lru_conv, cache_k, cache_v):
    bsz = x_sample.shape[0]
    x = x_sample.reshape(bsz, D_MODEL)
    states = [[] for _ in range(6)]
    for l in range(DEPTH):
        lw = lws[l]
        proj = _in_proj(x, lw['w_in'])
        y_all, sbuf, lbuf, hl, kc, vc, h = _mixer_decode(
            proj, lw, jnp.transpose(state_ssd_conv[l], (1, 0, 2)), jnp.transpose(state_lru_conv[l], (1, 0, 2)),
            state_lru[l], cache_k[l].reshape(bsz, WINDOW, KV_DIM), cache_v[l].reshape(bsz, WINDOW, KV_DIM),
            state_ssd[l])
        states[0].append(h)
        states[1].append(jnp.transpose(sbuf, (1, 0, 2)))
        states[2].append(hl)
        states[3].append(jnp.transpose(lbuf, (1, 0, 2)))
        states[4].append(kc.reshape(bsz, WINDOW, ATTN_KV_HEADS, ATTN_HEAD_DIM))
        states[5].append(vc.reshape(bsz, WINDOW, ATTN_KV_HEADS, ATTN_HEAD_DIM))
        x = _out_proj_ln(y_all, x, lw['w_out'], lw['ln1_g'], lw['ln1_b'])
        x = _channel_mixer(l, x, lw)
    return x.reshape(bsz, 1, D_MODEL), [jnp.stack(s, axis=0) for s in states]


def kernel(x_prompt, x_sample, state_ssd, state_ssd_conv, state_lru, state_lru_conv, cache_swa_k, cache_swa_v, w_in, ssd_conv_w, ssd_conv_b, ssd_dt_bias, ssd_a_log, ssd_d, ssd_norm_g, lru_conv_w, lru_conv_b, lru_wa, lru_ba, lru_wx, lru_bx, lru_lambda, attn_sinks, w_out, ln1_g, ln1_b, ln2_g, ln2_b, ffn_w_gate, ffn_w_up, ffn_w_down, moe_router, moe_w_gate, moe_w_up, moe_w_down):
    p = dict(w_in=w_in, ssd_conv_w=ssd_conv_w, ssd_conv_b=ssd_conv_b, ssd_dt_bias=ssd_dt_bias,
             ssd_a_log=ssd_a_log, ssd_d=ssd_d, ssd_norm_g=ssd_norm_g, lru_conv_w=lru_conv_w,
             lru_conv_b=lru_conv_b, lru_wa=lru_wa, lru_ba=lru_ba, lru_wx=lru_wx, lru_bx=lru_bx,
             lru_lambda=lru_lambda, attn_sinks=attn_sinks, w_out=w_out, ln1_g=ln1_g, ln1_b=ln1_b,
             ln2_g=ln2_g, ln2_b=ln2_b, ffn_w_gate=ffn_w_gate, ffn_w_up=ffn_w_up,
             ffn_w_down=ffn_w_down, moe_router=moe_router, moe_w_gate=moe_w_gate,
             moe_w_up=moe_w_up, moe_w_down=moe_w_down)
    lws = [_prep_layer(l, p) for l in range(DEPTH)]
    y_prompt, p_st = _prompt_trunk(x_prompt, lws, p)
    y_sample, s_st = _sample_trunk(x_sample, lws, p, state_ssd, state_ssd_conv, state_lru, state_lru_conv,
                                   cache_swa_k, cache_swa_v)
    return (y_prompt, y_sample) + tuple(p_st) + tuple(s_st)
```

```python
import functools
import math

import jax
import jax.numpy as jnp
from jax import lax
from jax.experimental import pallas as pl
from jax.experimental.pallas import tpu as pltpu

F32 = jnp.float32
BF16 = jnp.bfloat16

D_MODEL = 1024
DEPTH = 2
CONV_WIDTH = 4
SSD_HEADS = 16
SSD_HEAD_DIM = 64
SSD_WIDTH = SSD_HEADS * SSD_HEAD_DIM
SSD_GROUPS = 2
SSD_HPG = SSD_HEADS // SSD_GROUPS
SSD_STATE = 64
SSD_GN = SSD_GROUPS * SSD_STATE
SSD_CONV_DIM = SSD_WIDTH + 2 * SSD_GN
LRU_WIDTH = 512
LRU_BLOCKS = 8
LRU_BLOCK_DIM = LRU_WIDTH // LRU_BLOCKS
LRU_C = 8.0
ATTN_Q_HEADS = 8
ATTN_KV_HEADS = 2
ATTN_GROUP = ATTN_Q_HEADS // ATTN_KV_HEADS
ATTN_HEAD_DIM = 64
ATTN_WIDTH = ATTN_Q_HEADS * ATTN_HEAD_DIM
KV_DIM = ATTN_KV_HEADS * ATTN_HEAD_DIM
WINDOW = 128
ATTN_SCALE = ATTN_HEAD_DIM ** -0.5
MIX_WIDTH = SSD_WIDTH + LRU_WIDTH + ATTN_WIDTH
N_EXPERTS = 8
TOP_K = 2
ALPHA = (2.0 * DEPTH) ** 0.25
LN_EPS = 1e-5
RMS_EPS = 1e-5

LANES = 128
SUBLANES = 8
VMEM_LIMIT_BYTES = 56 * 1024 * 1024

OFF_Z = 0
OFF_XBC = OFF_Z + SSD_WIDTH
OFF_LGATE = OFF_XBC + SSD_CONV_DIM
OFF_LX = OFF_LGATE + LRU_WIDTH
OFF_Q = OFF_LX + LRU_WIDTH
OFF_K = OFF_Q + ATTN_WIDTH
OFF_V = OFF_K + KV_DIM
OFF_DT = OFF_V + KV_DIM
PROJ_WIDTH = OFF_DT + LANES

CHUNK = 128
TOKEN_TILE = 512
PROJ_COL_TILE = PROJ_WIDTH // 3
FF_TILE = 512
TOKEN_TILE_SHAPE = (SUBLANES, LANES)
assert SUBLANES * LANES == D_MODEL


def _cparams(*sem):
    return pltpu.CompilerParams(dimension_semantics=sem, vmem_limit_bytes=VMEM_LIMIT_BYTES)


def _dot(a, b):
    return jnp.dot(a.astype(BF16), b.astype(BF16), preferred_element_type=F32)


def _dot_nt(a, b):
    return lax.dot_general(a.astype(BF16), b.astype(BF16), (((1,), (1,)), ((), ())),
                           preferred_element_type=F32)


def _split3(x):
    hi = x.astype(BF16)
    r1 = x - hi.astype(F32)
    mid = r1.astype(BF16)
    lo = (r1 - mid.astype(F32)).astype(BF16)
    return hi, mid, lo


def _select_dot_rhs(x, sel):
    hi, mid, lo = _split3(x)
    d = lambda t: jnp.dot(t, sel, preferred_element_type=F32)
    return d(hi) + d(mid) + d(lo)


def _select_dot_lhs(sel, x):
    hi, mid, lo = _split3(x)
    d = lambda t: jnp.dot(sel, t, preferred_element_type=F32)
    return d(hi) + d(mid) + d(lo)


def _silu(x):
    return x * jax.nn.sigmoid(x)


def _layer_norm(h, g, b):
    mu = jnp.mean(h, axis=-1, keepdims=True)
    hc = h - mu
    var = jnp.mean(hc * hc, axis=-1, keepdims=True)
    return hc * lax.rsqrt(var + LN_EPS) * g + b


def _inproj_kernel(x_ref, w_ref, o_ref):
    o_ref[...] = jnp.dot(x_ref[...].astype(BF16), w_ref[...], preferred_element_type=F32)


def _in_proj(x, w):
    m = x.shape[0]
    tm = min(TOKEN_TILE, m)
    grid = (PROJ_WIDTH // PROJ_COL_TILE, m // tm)
    return pl.pallas_call(
        _inproj_kernel,
        grid=grid,
        in_specs=[pl.BlockSpec((tm, D_MODEL), lambda j, i: (i, 0)),
                  pl.BlockSpec((D_MODEL, PROJ_COL_TILE), lambda j, i: (0, j))],
        out_specs=pl.BlockSpec((tm, PROJ_COL_TILE), lambda j, i: (i, j)),
        out_shape=jax.ShapeDtypeStruct((m, PROJ_WIDTH), F32),
        compiler_params=_cparams("arbitrary", "arbitrary"),
        name="in_proj",
    )(x, w)


def _outproj_ln_kernel(y_ref, x_ref, w_ref, g_ref, b_ref, o_ref):
    mix = jnp.dot(y_ref[...].astype(BF16), w_ref[...], preferred_element_type=F32)
    o_ref[...] = _layer_norm(ALPHA * x_ref[...] + mix, g_ref[...], b_ref[...])


def _out_proj_ln(y_all, x, w, g, b):
    m = x.shape[0]
    tm = min(TOKEN_TILE, m)
    row = lambda i: (i, 0)
    fix = lambda i: (0, 0)
    return pl.pallas_call(
        _outproj_ln_kernel,
        grid=(m // tm,),
        in_specs=[pl.BlockSpec((tm, MIX_WIDTH), row), pl.BlockSpec((tm, D_MODEL), row),
                  pl.BlockSpec((MIX_WIDTH, D_MODEL), fix),
                  pl.BlockSpec((1, D_MODEL), fix), pl.BlockSpec((1, D_MODEL), fix)],
        out_specs=pl.BlockSpec((tm, D_MODEL), row),
        out_shape=jax.ShapeDtypeStruct((m, D_MODEL), F32),
        compiler_params=_cparams("arbitrary"),
        name="out_proj_ln",
    )(y_all, x, w, g, b)


def _swiglu_kernel(te_ref, tv_ref, x_ref, wg_ref, wu_ref, wd_ref, g_ref, b_ref, o_ref, acc_ref, xb_ref, *, fuse_ln):
    i = pl.program_id(0)
    f = pl.program_id(1)
    nf = pl.num_programs(1)
    live = tv_ref[i] > 0

    @pl.when(live & (f == 0))
    def _():
        xb_ref[...] = x_ref[...].reshape(xb_ref.shape).astype(BF16)

    @pl.when(live)
    def _():
        xb = xb_ref[...]
        hg = jnp.dot(xb, wg_ref[0], preferred_element_type=F32)
        hu = jnp.dot(xb, wu_ref[0], preferred_element_type=F32)
        part = jnp.dot((_silu(hg) * hu).astype(BF16), wd_ref[0], preferred_element_type=F32)

        @pl.when(f == 0)
        def _():
            acc_ref[...] = part

        @pl.when(f > 0)
        def _():
            acc_ref[...] += part

    @pl.when(live & (f == nf - 1))
    def _():
        if fuse_ln:
            o_ref[...] = _layer_norm(ALPHA * x_ref[...] + acc_ref[...], g_ref[...], b_ref[...])
        else:
            o_ref[...] = acc_ref[...].reshape(o_ref.shape)

    @pl.when(jnp.logical_not(live) & (f == nf - 1))
    def _():
        o_ref[...] = jnp.zeros_like(o_ref)


def _swiglu(x, tile_expert, tile_live, wg, wu, wd, g, b, *, tm, fuse_ln):
    m = x.shape[0]
    assert m % tm == 0 and tile_expert.shape[0] == m // tm and tile_live.shape[0] == m // tm
    row_block = (tm,) + x.shape[1:]
    row_map = (lambda i, f, te, tv: (i, 0)) if x.ndim == 2 else (lambda i, f, te, tv: (i, 0, 0))
    d_ff = wg.shape[2]
    ff = FF_TILE if d_ff % FF_TILE == 0 else d_ff // 2
    assert d_ff % ff == 0 and ff % LANES == 0
    nf = d_ff // ff
    fsel = lambda f, tv_i: f * tv_i + (nf - 1) * (1 - tv_i)
    fix = lambda i, f, te, tv: (0, 0)
    grid_spec = pltpu.PrefetchScalarGridSpec(
        num_scalar_prefetch=2,
        grid=(m // tm, nf),
        in_specs=[pl.BlockSpec(row_block, row_map),
                  pl.BlockSpec((1, D_MODEL, ff), lambda i, f, te, tv: (te[i], 0, fsel(f, tv[i]))),
                  pl.BlockSpec((1, D_MODEL, ff), lambda i, f, te, tv: (te[i], 0, fsel(f, tv[i]))),
                  pl.BlockSpec((1, ff, D_MODEL), lambda i, f, te, tv: (te[i], fsel(f, tv[i]), 0)),
                  pl.BlockSpec((1, D_MODEL), fix), pl.BlockSpec((1, D_MODEL), fix)],
        out_specs=pl.BlockSpec(row_block, row_map),
        scratch_shapes=[pltpu.VMEM((tm, D_MODEL), F32), pltpu.VMEM((tm, D_MODEL), BF16)],
    )
    return pl.pallas_call(
        functools.partial(_swiglu_kernel, fuse_ln=fuse_ln),
        grid_spec=grid_spec,
        out_shape=jax.ShapeDtypeStruct(x.shape, F32),
        compiler_params=_cparams("arbitrary", "arbitrary"),
        name="swiglu_ln" if fuse_ln else "swiglu_grouped",
    )(tile_expert, tile_live, x, wg, wu, wd, g, b)


def _router_kernel(x_ref, wr_ref, idx_ref, gate_ref, xt_ref):
    x = x_ref[...]
    xt_ref[...] = x.reshape(xt_ref.shape)
    logits = jnp.dot(x.astype(BF16), wr_ref[...], preferred_element_type=F32)
    lane = lax.broadcasted_iota(jnp.int32, logits.shape, 1)
    logits = jnp.where(lane < N_EXPERTS, logits, -jnp.inf)
    v1 = jnp.max(logits, axis=-1, keepdims=True)
    i1 = jnp.min(jnp.where(logits == v1, lane, LANES), axis=-1, keepdims=True)
    rest = jnp.where(lane == i1, -jnp.inf, logits)
    v2 = jnp.max(rest, axis=-1, keepdims=True)
    i2 = jnp.min(jnp.where(rest == v2, lane, LANES), axis=-1, keepdims=True)
    e2 = jnp.exp(v2 - v1)
    den = 1.0 + e2
    idx_ref[...] = jnp.where(lane == 0, i1, jnp.where(lane == 1, i2, 0))
    gate_ref[...] = jnp.where(lane == 0, 1.0 / den, jnp.where(lane == 1, e2 / den, 0.0))


def _router(x, wr):
    m = x.shape[0]
    tm = min(TOKEN_TILE, m)
    row = lambda i: (i, 0)
    return pl.pallas_call(
        _router_kernel,
        grid=(m // tm,),
        in_specs=[pl.BlockSpec((tm, D_MODEL), row), pl.BlockSpec((D_MODEL, LANES), lambda i: (0, 0))],
        out_specs=[pl.BlockSpec((tm, LANES), row), pl.BlockSpec((tm, LANES), row),
                   pl.BlockSpec((tm,) + TOKEN_TILE_SHAPE, lambda i: (i, 0, 0))],
        out_shape=[jax.ShapeDtypeStruct((m, LANES), jnp.int32), jax.ShapeDtypeStruct((m, LANES), F32),
                   jax.ShapeDtypeStruct((m,) + TOKEN_TILE_SHAPE, F32)],
        compiler_params=_cparams("arbitrary"),
        name="router_top2",
    )(x, wr)


def _row_gather_kernel(idx_ref, src_ref, dst_ref, sem, *, rows):
    def start(r, carry):
        pltpu.make_async_copy(src_ref.at[pl.ds(idx_ref[0, 0, r], 1)], dst_ref.at[pl.ds(r, 1)], sem).start()
        return carry

    lax.fori_loop(0, rows, start, 0, unroll=8)
    pltpu.make_async_copy(src_ref.at[pl.ds(0, rows)], dst_ref, sem).wait()


def _row_gather(src, idx, rows):
    n_rows = idx.shape[0]
    assert n_rows % rows == 0
    steps = n_rows // rows
    return pl.pallas_call(
        functools.partial(_row_gather_kernel, rows=rows),
        grid=(steps,),
        in_specs=[pl.BlockSpec((1, 1, rows), lambda i: (i, 0, 0), memory_space=pltpu.SMEM),
                  pl.BlockSpec(memory_space=pl.ANY)],
        out_specs=pl.BlockSpec((rows,) + src.shape[1:], lambda i: (i, 0, 0)),
        out_shape=jax.ShapeDtypeStruct((n_rows,) + src.shape[1:], src.dtype),
        scratch_shapes=[pltpu.SemaphoreType.DMA],
        compiler_params=_cparams("arbitrary"),
        name="row_gather",
    )(idx.reshape(steps, 1, rows), src)


def _combine_ln_kernel(x_ref, y_ref, gate_ref, g_ref, b_ref, o_ref):
    gate = gate_ref[...]
    x = x_ref[...]
    f = gate[:, 0:1] * y_ref[:, 0].reshape(x.shape) + gate[:, 1:2] * y_ref[:, 1].reshape(x.shape)
    o_ref[...] = _layer_norm(ALPHA * x + f, g_ref[...], b_ref[...])


def _combine_ln(x, y_pairs, gates, g, b):
    m = x.shape[0]
    tm = min(TOKEN_TILE, m)
    row = lambda i: (i, 0)
    fix = lambda i: (0, 0)
    return pl.pallas_call(
        _combine_ln_kernel,
        grid=(m // tm,),
        in_specs=[pl.BlockSpec((tm, D_MODEL), row),
                  pl.BlockSpec((tm, TOP_K) + TOKEN_TILE_SHAPE, lambda i: (i, 0, 0, 0)),
                  pl.BlockSpec((tm, LANES), row),
                  pl.BlockSpec((1, D_MODEL), fix), pl.BlockSpec((1, D_MODEL), fix)],
        out_specs=pl.BlockSpec((tm, D_MODEL), row),
        out_shape=jax.ShapeDtypeStruct((m, D_MODEL), F32),
        compiler_params=_cparams("arbitrary"),
        name="moe_combine_ln",
    )(x, y_pairs, gates, g, b)


def _moe_ffn_ln(x, wr, wg, wu, wd, g, b):
    m = x.shape[0]
    tm = min(TOKEN_TILE, m)
    top_i, gates, x_tiles = _router(x, wr)
    e_flat = top_i[:, :TOP_K].reshape(-1)
    onehot = (e_flat[:, None] == jnp.arange(N_EXPERTS, dtype=jnp.int32)[None, :]).astype(jnp.int32)
    rank = jnp.take_along_axis(jnp.cumsum(onehot, axis=0), e_flat[:, None], axis=1)[:, 0] - 1
    counts = jnp.sum(onehot, axis=0)
    padded = ((counts + tm - 1) // tm) * tm
    ends = jnp.cumsum(padded)
    starts = ends - padded
    pos = starts[e_flat] + rank
    n_pad = TOP_K * m + N_EXPERTS * tm
    n_tiles = n_pad // tm
    row_token = jnp.zeros((n_pad,), jnp.int32).at[pos].set(jnp.arange(TOP_K * m, dtype=jnp.int32) // TOP_K)
    tile_start = jnp.arange(n_tiles, dtype=jnp.int32) * tm
    tile_expert = jnp.minimum(jnp.sum((tile_start[:, None] >= ends[None, :]).astype(jnp.int32), axis=1),
                              N_EXPERTS - 1).astype(jnp.int32)
    tile_live = (tile_start < ends[N_EXPERTS - 1]).astype(jnp.int32)
    xs = _row_gather(x_tiles, row_token, tm)
    ys = _swiglu(xs, tile_expert, tile_live, wg, wu, wd, g, b, tm=tm, fuse_ln=False)
    y_pairs = _row_gather(ys, pos.astype(jnp.int32), tm).reshape((m, TOP_K) + TOKEN_TILE_SHAPE)
    return _combine_ln(x, y_pairs, gates, g, b)


def _shift_rows(x, tail, sh):
    if sh == 0:
        return x
    row8 = lax.broadcasted_iota(jnp.int32, (SUBLANES, 1), 0)
    xr = pltpu.roll(x, sh, 0)
    head = jnp.where(row8 < sh, pltpu.roll(tail, sh, 0), xr[0:SUBLANES])
    return jnp.concatenate([head, xr[SUBLANES:]], axis=0)


def _causal_conv(x, tail, w_ref, b_ref):
    acc = b_ref[...] + _shift_rows(x, tail, CONV_WIDTH - 1) * w_ref[0:1, :]
    for j in range(1, CONV_WIDTH):
        acc = acc + _shift_rows(x, tail, CONV_WIDTH - 1 - j) * w_ref[j:j + 1, :]
    return acc


def _lru_gates(xl, wax_ref, bax_ref, lam_ref):
    gates = jnp.dot(xl.astype(BF16), wax_ref[...], preferred_element_type=F32) + bax_ref[...]
    r = jax.nn.sigmoid(gates[:, :LRU_WIDTH])
    i = jax.nn.sigmoid(gates[:, LRU_WIDTH:])
    log_a = -LRU_C * r * jax.nn.softplus(-lam_ref[...])
    a = jnp.exp(log_a)
    th = jnp.tanh(log_a)
    u = jnp.sqrt(-2.0 * th / (1.0 - th)) * (i * xl)
    return a, u


def _sink_softmax(s, sink):
    m = jnp.maximum(jnp.max(s, axis=-1, keepdims=True), sink)
    e = jnp.exp(s - m)
    den = jnp.sum(e, axis=-1, keepdims=True) + jnp.exp(sink - m)
    return e, den


def _mixer_prompt_kernel(sink_ref, proj_ref, cw_ref, cb_ref, dtb_ref, alog_ref, drow_ref, ng_ref, rexp_ref,
                         gmask_ref, lcw_ref, lcb_ref, wax_ref, bax_ref, lam_ref,
                         y_ref, s_out_ref, hl_out_ref,
                         xtail, ltail, s_scr, hl_scr, kprev, vprev):
    c = pl.program_id(1)
    nc = pl.num_programs(1)
    t = CHUNK

    @pl.when(c == 0)
    def _():
        xtail[...] = jnp.zeros_like(xtail)
        ltail[...] = jnp.zeros_like(ltail)
        s_scr[...] = jnp.zeros_like(s_scr)
        hl_scr[...] = jnp.zeros_like(hl_scr)
        kprev[...] = jnp.zeros_like(kprev)
        vprev[...] = jnp.zeros_like(vprev)

    row = lax.broadcasted_iota(jnp.int32, (t, 1), 0)
    col = lax.broadcasted_iota(jnp.int32, (1, t), 1)
    lane = lax.broadcasted_iota(jnp.int32, (1, LANES), 1)
    causal = row >= col

    z = proj_ref[:, OFF_Z:OFF_Z + SSD_WIDTH]
    xbc_raw = proj_ref[:, OFF_XBC:OFF_XBC + SSD_CONV_DIM]
    xbc = _silu(_causal_conv(xbc_raw, xtail[...], cw_ref, cb_ref))
    xtail[...] = xbc_raw[t - SUBLANES:t]
    xs = xbc[:, :SSD_WIDTH]
    bm = xbc[:, SSD_WIDTH:SSD_WIDTH + SSD_GN]
    cm = xbc[:, SSD_WIDTH + SSD_GN:]
    dt = jax.nn.softplus(proj_ref[:, OFF_DT:OFF_DT + LANES] + dtb_ref[...])
    da = dt * (-jnp.exp(alog_ref[...]))
    acum = _select_dot_lhs(causal.astype(BF16), da)
    acum_t = acum.T
    rexp = rexp_ref[...]
    acum_e = _select_dot_rhs(acum, rexp)
    dt_e = _select_dot_rhs(dt, rexp)
    xdt = xs * dt_e
    exp_e = jnp.exp(acum_e)
    last_e = acum_e[t - 1:t, :]
    xw = xdt * jnp.exp(last_e - acum_e)
    bmb = bm.astype(BF16)

    pairs = []
    for g in range(SSD_GROUPS):
        gsel = (lane >= g * SSD_STATE) & (lane < (g + 1) * SSD_STATE)
        cb = _dot_nt(jnp.where(gsel, cm, 0.0), bmb)
        for pc in range(SSD_HPG // 2):
            p2 = g * (SSD_HPG // 2) + pc
            ms = []
            for j in (2 * p2, 2 * p2 + 1):
                seg = acum[:, j:j + 1] - acum_t[j:j + 1, :]
                dec = jnp.exp(jnp.where(causal, seg, -jnp.inf))
                ms.append((cb * dec).astype(BF16))
            lhs = jnp.concatenate(ms, axis=1)
            xp = xdt[:, LANES * p2:LANES * (p2 + 1)]
            rhs = jnp.concatenate([jnp.where(lane < SSD_HEAD_DIM, xp, 0.0),
                                   jnp.where(lane >= SSD_HEAD_DIM, xp, 0.0)], axis=0).astype(BF16)
            pairs.append(jnp.dot(lhs, rhs, preferred_element_type=F32))
    y_diag = jnp.concatenate(pairs, axis=1)

    s_old = s_scr[...]
    y_off = _dot(cm, s_old) * exp_e
    btx = _dot(bm.T, xw)
    s_scr[...] = s_old * exp_e[t - 1:t, :] + jnp.where(gmask_ref[...] > 0.0, btx, 0.0)

    y = (y_diag + y_off + drow_ref[...] * xs) * _silu(z)
    half = SSD_WIDTH // SSD_GROUPS
    normed = []
    for g in range(SSD_GROUPS):
        seg = y[:, g * half:(g + 1) * half]
        normed.append(seg * lax.rsqrt(jnp.mean(seg * seg, axis=-1, keepdims=True) + RMS_EPS))
    y_ssd = jnp.concatenate(normed, axis=1) * ng_ref[...]

    lx_raw = proj_ref[:, OFF_LX:OFF_LX + LRU_WIDTH]
    xl = _causal_conv(lx_raw, ltail[...], lcw_ref, lcb_ref)
    ltail[...] = lx_raw[t - SUBLANES:t]
    a_s, u_s = _lru_gates(xl, wax_ref, bax_ref, lam_ref)
    d = 1
    while d < t:
        a_sh = jnp.where(row >= d, pltpu.roll(a_s, d, 0), 1.0)
        u_sh = jnp.where(row >= d, pltpu.roll(u_s, d, 0), 0.0)
        u_s = a_s * u_sh + u_s
        a_s = a_s * a_sh
        d *= 2
    h_seq = a_s * hl_scr[...] + u_s
    hl_scr[...] = h_seq[t - 1:t, :]
    y_lru = h_seq * jax.nn.gelu(proj_ref[:, OFF_LGATE:OFF_LGATE + LRU_WIDTH])

    k = proj_ref[:, OFF_K:OFF_K + KV_DIM]
    v = proj_ref[:, OFF_V:OFF_V + KV_DIM]
    kc = jnp.concatenate([kprev[...], k], axis=0).astype(BF16)
    vc = jnp.concatenate([vprev[...], v], axis=0).astype(BF16)
    kprev[...] = k
    vprev[...] = v
    si = lax.broadcasted_iota(jnp.int32, (1, 2 * t), 1)
    first_key = jnp.where(c > 0, 0, t)
    valid = (si > row) & (si <= row + t) & (si >= first_key)
    blocks = []
    for g in range(ATTN_GROUP):
        qb = proj_ref[:, OFF_Q + LANES * g:OFF_Q + LANES * (g + 1)]
        halves = []
        for kv in range(ATTN_KV_HEADS):
            hsel = (lane >= kv * ATTN_HEAD_DIM) & (lane < (kv + 1) * ATTN_HEAD_DIM)
            s = _dot_nt(jnp.where(hsel, qb, 0.0), kc) * ATTN_SCALE
            s = jnp.where(valid, s, -jnp.inf)
            e, den = _sink_softmax(s, sink_ref[kv * ATTN_GROUP + g])
            halves.append(jnp.dot((e / den).astype(BF16), vc, preferred_element_type=F32))
        blocks.append(jnp.where(lane < ATTN_HEAD_DIM, halves[0], halves[1]))
    y_attn = jnp.concatenate(blocks, axis=1)

    y_ref[...] = jnp.concatenate([y_ssd, y_lru, y_attn], axis=1).astype(y_ref.dtype)

    @pl.when(c == nc - 1)
    def _():
        s_out_ref[0] = s_scr[...]
        hl_out_ref[0] = hl_scr[...]


def _mixer_prompt(proj, lw, bsz, seq):
    nc = seq // CHUNK
    fix = lambda b, c: (0, 0)
    vec = lambda n: pl.BlockSpec((1, n), fix)
    in_specs = [
        pl.BlockSpec(memory_space=pltpu.SMEM),
        pl.BlockSpec((CHUNK, PROJ_WIDTH), lambda b, c: (b * nc + c, 0)),
        pl.BlockSpec((CONV_WIDTH, SSD_CONV_DIM), fix), vec(SSD_CONV_DIM),
        vec(LANES), vec(LANES), vec(SSD_WIDTH), vec(SSD_WIDTH),
        pl.BlockSpec((LANES, SSD_WIDTH), fix), pl.BlockSpec((SSD_GN, SSD_WIDTH), fix),
        pl.BlockSpec((CONV_WIDTH, LRU_WIDTH), fix), vec(LRU_WIDTH),
        pl.BlockSpec((LRU_WIDTH, 2 * LRU_WIDTH), fix), vec(2 * LRU_WIDTH), vec(LRU_WIDTH),
    ]
    out_specs = [
        pl.BlockSpec((CHUNK, MIX_WIDTH), lambda b, c: (b * nc + c, 0)),
        pl.BlockSpec((1, SSD_GN, SSD_WIDTH), lambda b, c: (b, 0, 0)),
        pl.BlockSpec((1, 1, LRU_WIDTH), lambda b, c: (b, 0, 0)),
    ]
    out_shape = [
        jax.ShapeDtypeStruct((bsz * seq, MIX_WIDTH), BF16),
        jax.ShapeDtypeStruct((bsz, SSD_GN, SSD_WIDTH), F32),
        jax.ShapeDtypeStruct((bsz, 1, LRU_WIDTH), F32),
    ]
    scratch = [
        pltpu.VMEM((SUBLANES, SSD_CONV_DIM), F32), pltpu.VMEM((SUBLANES, LRU_WIDTH), F32),
        pltpu.VMEM((SSD_GN, SSD_WIDTH), F32), pltpu.VMEM((1, LRU_WIDTH), F32),
        pltpu.VMEM((CHUNK, KV_DIM), F32), pltpu.VMEM((CHUNK, KV_DIM), F32),
    ]
    return pl.pallas_call(
        _mixer_prompt_kernel,
        grid=(bsz, nc),
        in_specs=in_specs,
        out_specs=out_specs,
        out_shape=out_shape,
        scratch_shapes=scratch,
        compiler_params=_cparams("arbitrary", "arbitrary"),
        name="mixer_prompt",
    )(lw['sinks'], proj, lw['ssd_conv_w'], lw['ssd_conv_b'], lw['dt_bias'], lw['a_log'], lw['d_row'],
      lw['norm_g'], lw['head_expand'], lw['group_mask'], lw['lru_conv_w'], lw['lru_conv_b'],
      lw['lru_wax'], lw['lru_bax'], lw['lru_lambda'])


DEC_ROWS = SUBLANES


def _conv_step(x, buf_ref, w_ref, b_ref):
    acc = b_ref[...] + buf_ref[0] * w_ref[0:1, :]
    for j in range(1, CONV_WIDTH - 1):
        acc = acc + buf_ref[j] * w_ref[j:j + 1, :]
    return acc + x * w_ref[CONV_WIDTH - 1:CONV_WIDTH, :]


def _mixer_decode_kernel(sink_ref, proj_ref, sbuf_ref, lbuf_ref, hl_ref, kc_ref, vc_ref, h_ref,
                         cw_ref, cb_ref, dtb_ref, alog_ref, drow_ref, ng_ref, rexp_ref,
                         lcw_ref, lcb_ref, wax_ref, bax_ref, lam_ref,
                         y_ref, sbuf_out, lbuf_out, hl_out, kc_out, vc_out, h_out):
    nb = DEC_ROWS
    row8 = lax.broadcasted_iota(jnp.int32, (nb, 1), 0)
    lane = lax.broadcasted_iota(jnp.int32, (1, LANES), 1)
    wrow = lax.broadcasted_iota(jnp.int32, (WINDOW, 1), 0)

    z = proj_ref[:, OFF_Z:OFF_Z + SSD_WIDTH]
    xbc_raw = proj_ref[:, OFF_XBC:OFF_XBC + SSD_CONV_DIM]
    xbc = _silu(_conv_step(xbc_raw, sbuf_ref, cw_ref, cb_ref))
    for j in range(CONV_WIDTH - 2):
        sbuf_out[j] = sbuf_ref[j + 1]
    sbuf_out[CONV_WIDTH - 2] = xbc_raw
    xs = xbc[:, :SSD_WIDTH]
    bm = xbc[:, SSD_WIDTH:SSD_WIDTH + SSD_GN]
    cm = xbc[:, SSD_WIDTH + SSD_GN:]
    dt = jax.nn.softplus(proj_ref[:, OFF_DT:OFF_DT + LANES] + dtb_ref[...])
    da = dt * (-jnp.exp(alog_ref[...]))
    rexp = rexp_ref[...]
    decay_e = jnp.exp(_select_dot_rhs(da, rexp))
    xdt = xs * _select_dot_rhs(dt, rexp)
    cols = jnp.concatenate([xdt, decay_e, jnp.zeros((LANES - 2 * nb, SSD_WIDTH), F32)], axis=0)
    cols_t = [cols[:, LANES * p2:LANES * (p2 + 1)].T for p2 in range(SSD_HEADS // 2)]
    bm_g = [bm[:, :SSD_STATE], pltpu.roll(bm, SSD_STATE, 1)[:, :SSD_STATE]]
    cm_g = [cm[:, :SSD_STATE], pltpu.roll(cm, SSD_STATE, 1)[:, :SSD_STATE]]
    half = SSD_WIDTH // SSD_GROUPS
    y_off = [jnp.zeros((nb, half), F32) for _ in range(SSD_GROUPS)]
    for bb in range(nb):
        for g in range(SSD_GROUPS):
            h_g = h_ref[bb, g * SSD_HPG:(g + 1) * SSD_HPG].reshape(half, SSD_STATE)
            y_off[g] = jnp.where(row8 == bb, _dot_nt(cm_g[g], h_g), y_off[g])
        for p2 in range(SSD_HEADS // 2):
            g = p2 // (SSD_HPG // 2)
            h_pair = h_ref[bb, 2 * p2:2 * p2 + 2].reshape(2 * SSD_HEAD_DIM, SSD_STATE)
            xcol = cols_t[p2][:, bb:bb + 1]
            dcol = cols_t[p2][:, nb + bb:nb + bb + 1]
            h_new = h_pair * dcol + xcol * bm_g[g][bb:bb + 1, :]
            h_out[bb, 2 * p2:2 * p2 + 2] = h_new.reshape(2, SSD_HEAD_DIM, SSD_STATE)
    cbv = cm * bm
    ys = []
    for g in range(SSD_GROUPS):
        gsel = (lane >= g * SSD_STATE) & (lane < (g + 1) * SSD_STATE)
        cb_g = jnp.sum(jnp.where(gsel, cbv, 0.0), axis=-1, keepdims=True)
        sl = slice(g * half, (g + 1) * half)
        ys.append(cb_g * xdt[:, sl] + y_off[g] * decay_e[:, sl])
    y = (jnp.concatenate(ys, axis=1) + drow_ref[...] * xs) * _silu(z)
    normed = []
    for g in range(SSD_GROUPS):
        seg = y[:, g * half:(g + 1) * half]
        normed.append(seg * lax.rsqrt(jnp.mean(seg * seg, axis=-1, keepdims=True) + RMS_EPS))
    y_ssd = jnp.concatenate(normed, axis=1) * ng_ref[...]

    lx_raw = proj_ref[:, OFF_LX:OFF_LX + LRU_WIDTH]
    xl = _conv_step(lx_raw, lbuf_ref, lcw_ref, lcb_ref)
    for j in range(CONV_WIDTH - 2):
        lbuf_out[j] = lbuf_ref[j + 1]
    lbuf_out[CONV_WIDTH - 2] = lx_raw
    a_s, u_s = _lru_gates(xl, wax_ref, bax_ref, lam_ref)
    h_lru = a_s * hl_ref[...] + u_s
    hl_out[...] = h_lru
    y_lru = h_lru * jax.nn.gelu(proj_ref[:, OFF_LGATE:OFF_LGATE + LRU_WIDTH])

    k_new = proj_ref[:, OFF_K:OFF_K + KV_DIM]
    v_new = proj_ref[:, OFF_V:OFF_V + KV_DIM]
    head_row_g = row8 // ATTN_KV_HEADS
    head_row_kv = row8 % ATTN_KV_HEADS
    lane_kv = lane // ATTN_HEAD_DIM
    sink_col = jnp.zeros((nb, 1), F32)
    for r in range(ATTN_Q_HEADS):
        sink_col = jnp.where(row8 == r, sink_ref[(r % ATTN_KV_HEADS) * ATTN_GROUP + r // ATTN_KV_HEADS], sink_col)
    y_blocks = [jnp.zeros((nb, LANES), F32) for _ in range(ATTN_GROUP)]
    for bb in range(nb):
        q_rows = jnp.zeros((ATTN_Q_HEADS, LANES), F32)
        for g in range(ATTN_GROUP):
            qb = proj_ref[bb:bb + 1, OFF_Q + LANES * g:OFF_Q + LANES * (g + 1)]
            q_rows = jnp.where((head_row_g == g) & (head_row_kv == lane_kv), qb, q_rows)
        kb = kc_ref[bb]
        vb = vc_ref[bb]
        kn = k_new[bb:bb + 1, :]
        vn = v_new[bb:bb + 1, :]
        s = _dot_nt(q_rows, kb) * ATTN_SCALE
        s = jnp.where(lane >= 1, s, -jnp.inf)
        s_new = jnp.sum(q_rows * kn, axis=-1, keepdims=True) * ATTN_SCALE
        m = jnp.maximum(jnp.maximum(jnp.max(s, axis=-1, keepdims=True), s_new), sink_col)
        e = jnp.exp(s - m)
        e_new = jnp.exp(s_new - m)
        den = jnp.sum(e, axis=-1, keepdims=True) + e_new + jnp.exp(sink_col - m)
        o = _dot(e / den, vb) + (e_new / den) * vn
        for g in range(ATTN_GROUP):
            blk = jnp.where(lane < ATTN_HEAD_DIM, o[2 * g:2 * g + 1, :], o[2 * g + 1:2 * g + 2, :])
            y_blocks[g] = jnp.where(row8 == bb, blk, y_blocks[g])
        kc_out[bb] = jnp.where(wrow == WINDOW - 1, kn, pltpu.roll(kb, WINDOW - 1, 0))
        vc_out[bb] = jnp.where(wrow == WINDOW - 1, vn, pltpu.roll(vb, WINDOW - 1, 0))
    y_attn = jnp.concatenate(y_blocks, axis=1)

    y_ref[...] = jnp.concatenate([y_ssd, y_lru, y_attn], axis=1)


def _mixer_decode(proj, lw, sbuf, lbuf, hl, kc, vc, h):
    bsz = proj.shape[0]
    nb = DEC_ROWS
    fix = lambda i: (0, 0)
    vec = lambda n: pl.BlockSpec((1, n), fix)
    rows = lambda n: pl.BlockSpec((nb, n), lambda i: (i, 0))
    buf = lambda n: pl.BlockSpec((CONV_WIDTH - 1, nb, n), lambda i: (0, i, 0))
    cache = pl.BlockSpec((nb, WINDOW, KV_DIM), lambda i: (i, 0, 0))
    state = pl.BlockSpec((nb, SSD_HEADS, SSD_HEAD_DIM, SSD_STATE), lambda i: (i, 0, 0, 0))
    in_specs = [
        pl.BlockSpec(memory_space=pltpu.SMEM),
        rows(PROJ_WIDTH), buf(SSD_CONV_DIM), buf(LRU_WIDTH), rows(LRU_WIDTH), cache, cache, state,
        pl.BlockSpec((CONV_WIDTH, SSD_CONV_DIM), fix), vec(SSD_CONV_DIM),
        vec(LANES), vec(LANES), vec(SSD_WIDTH), vec(SSD_WIDTH),
        pl.BlockSpec((LANES, SSD_WIDTH), fix),
        pl.BlockSpec((CONV_WIDTH, LRU_WIDTH), fix), vec(LRU_WIDTH),
        pl.BlockSpec((LRU_WIDTH, 2 * LRU_WIDTH), fix), vec(2 * LRU_WIDTH), vec(LRU_WIDTH),
    ]
    out_specs = [rows(MIX_WIDTH), buf(SSD_CONV_DIM), buf(LRU_WIDTH), rows(LRU_WIDTH), cache, cache, state]
    out_shape = [
        jax.ShapeDtypeStruct((bsz, MIX_WIDTH), F32),
        jax.ShapeDtypeStruct(sbuf.shape, F32), jax.ShapeDtypeStruct(lbuf.shape, F32),
        jax.ShapeDtypeStruct(hl.shape, F32), jax.ShapeDtypeStruct(kc.shape, F32),
        jax.ShapeDtypeStruct(vc.shape, F32), jax.ShapeDtypeStruct(h.shape, F32),
    ]
    return pl.pallas_call(
        _mixer_decode_kernel,
        grid=(bsz // nb,),
        in_specs=in_specs,
        out_specs=out_specs,
        out_shape=out_shape,
        compiler_params=_cparams("arbitrary"),
        name="mixer_decode",
    )(lw['sinks'], proj, sbuf, lbuf, hl, kc, vc, h,
      lw['ssd_conv_w'], lw['ssd_conv_b'], lw['dt_bias'], lw['a_log'], lw['d_row'], lw['norm_g'],
      lw['head_expand'], lw['lru_conv_w'], lw['lru_conv_b'], lw['lru_wax'], lw['lru_bax'], lw['lru_lambda'])


def _q_perm():
    g, kv, d = jnp.meshgrid(jnp.arange(ATTN_GROUP), jnp.arange(ATTN_KV_HEADS), jnp.arange(ATTN_HEAD_DIM),
                            indexing='ij')
    return (kv * ATTN_GROUP * ATTN_HEAD_DIM + g * ATTN_HEAD_DIM + d).reshape(-1)


def _block_diag(w):
    eye = jnp.eye(LRU_BLOCKS, dtype=w.dtype)
    return jnp.einsum('kij,kl->kilj', w, eye).reshape(LRU_WIDTH, LRU_WIDTH)


def _prep_layer(l, p):
    w_in = p['w_in'][l]
    o = 0
    segs = {}
    for name, size in (('z', SSD_WIDTH), ('xbc', SSD_CONV_DIM), ('dt', SSD_HEADS), ('lgate', LRU_WIDTH),
                       ('lx', LRU_WIDTH), ('q', ATTN_WIDTH), ('k', KV_DIM), ('v', KV_DIM)):
        segs[name] = w_in[:, o:o + size]
        o += size
    qp = _q_perm()
    w_in_p = jnp.concatenate(
        [segs['z'], segs['xbc'], segs['lgate'], segs['lx'], segs['q'][:, qp], segs['k'], segs['v'],
         jnp.pad(segs['dt'], ((0, 0), (0, LANES - SSD_HEADS)))], axis=1).astype(BF16)
    w_out = p['w_out'][l]
    attn_rows = w_out[SSD_WIDTH + LRU_WIDTH:]
    w_out_p = jnp.concatenate([w_out[:SSD_WIDTH + LRU_WIDTH], attn_rows[qp]], axis=0).astype(BF16)
    pad_h = lambda v: jnp.pad(v, (0, LANES - SSD_HEADS)).reshape(1, LANES)
    head_of_lane = jnp.arange(SSD_WIDTH) // SSD_HEAD_DIM
    head_expand = (jnp.arange(LANES)[:, None] == head_of_lane[None, :]).astype(BF16)
    group_mask = ((jnp.arange(SSD_GN)[:, None] // SSD_STATE) == (head_of_lane[None, :] // SSD_HPG)).astype(F32)
    return dict(
        w_in=w_in_p, w_out=w_out_p,
        ssd_conv_w=p['ssd_conv_w'][l], ssd_conv_b=p['ssd_conv_b'][l].reshape(1, -1),
        dt_bias=pad_h(p['ssd_dt_bias'][l]), a_log=pad_h(p['ssd_a_log'][l]),
        d_row=jnp.repeat(p['ssd_d'][l], SSD_HEAD_DIM).reshape(1, -1),
        norm_g=p['ssd_norm_g'][l].reshape(1, -1),
        head_expand=head_expand, group_mask=group_mask,
        lru_conv_w=p['lru_conv_w'][l], lru_conv_b=p['lru_conv_b'][l].reshape(1, -1),
        lru_wax=jnp.concatenate([_block_diag(p['lru_wa'][l]), _block_diag(p['lru_wx'][l])], axis=1).astype(BF16),
        lru_bax=jnp.concatenate([p['lru_ba'][l].reshape(-1), p['lru_bx'][l].reshape(-1)]).reshape(1, -1),
        lru_lambda=p['lru_lambda'][l].reshape(1, -1),
        sinks=p['attn_sinks'][l],
        channel=_prep_channel(l, p),
        ln1_g=p['ln1_g'][l].reshape(1, -1), ln1_b=p['ln1_b'][l].reshape(1, -1),
        ln2_g=p['ln2_g'][l].reshape(1, -1), ln2_b=p['ln2_b'][l].reshape(1, -1),
    )


def _prep_channel(l, p):
    i = l // 2
    if l % 2 == 0:
        return dict(wg=p['ffn_w_gate'][i][None].astype(BF16), wu=p['ffn_w_up'][i][None].astype(BF16),
                    wd=p['ffn_w_down'][i][None].astype(BF16))
    return dict(wr=jnp.pad(p['moe_router'][i], ((0, 0), (0, LANES - N_EXPERTS))).astype(BF16),
                wg=p['moe_w_gate'][i].astype(BF16), wu=p['moe_w_up'][i].astype(BF16),
                wd=p['moe_w_down'][i].astype(BF16))


def _channel_mixer(l, x, lw):
    cw = lw['channel']
    if l % 2 == 0:
        tm = min(TOKEN_TILE, x.shape[0])
        n_tiles = x.shape[0] // tm
        return _swiglu(x, jnp.zeros((n_tiles,), jnp.int32), jnp.ones((n_tiles,), jnp.int32),
                       cw['wg'], cw['wu'], cw['wd'], lw['ln2_g'], lw['ln2_b'], tm=tm, fuse_ln=True)
    return _moe_ffn_ln(x, cw['wr'], cw['wg'], cw['wu'], cw['wd'], lw['ln2_g'], lw['ln2_b'])


def _ssd_state_from_scratch_layout(s):
    bsz = s.shape[0]
    s6 = s.reshape(bsz, SSD_GROUPS, SSD_STATE, SSD_GROUPS, SSD_HPG, SSD_HEAD_DIM)
    diag = jnp.stack([s6[:, g, :, g] for g in range(SSD_GROUPS)], axis=1)
    return jnp.transpose(diag, (0, 1, 3, 4, 2)).reshape(bsz, SSD_HEADS, SSD_HEAD_DIM, SSD_STATE)


def _prompt_trunk(x_prompt, lws, p):
    bsz, seq, _ = x_prompt.shape
    x = x_prompt.reshape(bsz * seq, D_MODEL)
    states = [[] for _ in range(6)]
    for l in range(DEPTH):
        lw = lws[l]
        proj = _in_proj(x, lw['w_in'])
        y_all, s_fin, hl_fin = _mixer_prompt(proj, lw, bsz, seq)
        proj3 = proj.reshape(bsz, seq, PROJ_WIDTH)
        states[0].append(_ssd_state_from_scratch_layout(s_fin))
        states[1].append(proj3[:, seq - (CONV_WIDTH - 1):, OFF_XBC:OFF_XBC + SSD_CONV_DIM])
        states[2].append(hl_fin.reshape(bsz, LRU_WIDTH))
        states[3].append(proj3[:, seq - (CONV_WIDTH - 1):, OFF_LX:OFF_LX + LRU_WIDTH])
        states[4].append(proj3[:, seq - WINDOW:, OFF_K:OFF_K + KV_DIM].reshape(bsz, WINDOW, ATTN_KV_HEADS, ATTN_HEAD_DIM))
        states[5].append(proj3[:, seq - WINDOW:, OFF_V:OFF_V + KV_DIM].reshape(bsz, WINDOW, ATTN_KV_HEADS, ATTN_HEAD_DIM))
        x = _out_proj_ln(y_all, x, lw['w_out'], lw['ln1_g'], lw['ln1_b'])
        x = _channel_mixer(l, x, lw)
    return x.reshape(bsz, seq, D_MODEL), [jnp.stack(s, axis=0) for s in states]


def _sample_trunk(x_sample, lws, p, state_ssd, state_ssd_conv, state_lru, state_lru_conv, cache_k, cache_v):
    bsz = x_sample.shape[0]
    x = x_sample.reshape(bsz, D_MODEL)
    states = [[] for _ in range(6)]
    for l in range(DEPTH):
        lw = lws[l]
        proj = _in_proj(x, lw['w_in'])
        y_all, sbuf, lbuf, hl, kc, vc, h = _mixer_decode(
            proj, lw, jnp.transpose(state_ssd_conv[l], (1, 0, 2)), jnp.transpose(state_lru_conv[l], (1, 0, 2)),
            state_lru[l], cache_k[l].reshape(bsz, WINDOW, KV_DIM), cache_v[l].reshape(bsz, WINDOW, KV_DIM),
            state_ssd[l])
        states[0].append(h)
        states[1].append(jnp.transpose(sbuf, (1, 0, 2)))
        states[2].append(hl)
        states[3].append(jnp.transpose(lbuf, (1, 0, 2)))
        states[4].append(kc.reshape(bsz, WINDOW, ATTN_KV_HEADS, ATTN_HEAD_DIM))
        states[5].append(vc.reshape(bsz, WINDOW, ATTN_KV_HEADS, ATTN_HEAD_DIM))
        x = _out_proj_ln(y_all, x, lw['w_out'], lw['ln1_g'], lw['ln1_b'])
        x = _channel_mixer(l, x, lw)
    return x.reshape(bsz, 1, D_MODEL), [jnp.stack(s, axis=0) for s in states]


def kernel(x_prompt, x_sample, state_ssd, state_ssd_conv, state_lru, state_lru_conv, cache_swa_k, cache_swa_v, w_in, ssd_conv_w, ssd_conv_b, ssd_dt_bias, ssd_a_log, ssd_d, ssd_norm_g, lru_conv_w, lru_conv_b, lru_wa, lru_ba, lru_wx, lru_bx, lru_lambda, attn_sinks, w_out, ln1_g, ln1_b, ln2_g, ln2_b, ffn_w_gate, ffn_w_up, ffn_w_down, moe_router, moe_w_gate, moe_w_up, moe_w_down):
    p = dict(w_in=w_in, ssd_conv_w=ssd_conv_w, ssd_conv_b=ssd_conv_b, ssd_dt_bias=ssd_dt_bias,
             ssd_a_log=ssd_a_log, ssd_d=ssd_d, ssd_norm_g=ssd_norm_g, lru_conv_w=lru_conv_w,
             lru_conv_b=lru_conv_b, lru_wa=lru_wa, lru_ba=lru_ba, lru_wx=lru_wx, lru_bx=lru_bx,
             lru_lambda=lru_lambda, attn_sinks=attn_sinks, w_out=w_out, ln1_g=ln1_g, ln1_b=ln1_b,
             ln2_g=ln2_g, ln2_b=ln2_b, ffn_w_gate=ffn_w_gate, ffn_w_up=ffn_w_up,
             ffn_w_down=ffn_w_down, moe_router=moe_router, moe_w_gate=moe_w_gate,
             moe_w_up=moe_w_up, moe_w_down=moe_w_down)
    lws = [_prep_layer(l, p) for l in range(DEPTH)]
    y_prompt, p_st = _prompt_trunk(x_prompt, lws, p)
    y_sample, s_st = _sample_trunk(x_sample, lws, p, state_ssd, state_ssd_conv, state_lru, state_lru_conv,
                                   cache_swa_k, cache_swa_v)
    return (y_prompt, y_sample) + tuple(p_st) + tuple(s_st)
```

```python
import functools
import math

import jax
import jax.numpy as jnp
from jax import lax
from jax.experimental import pallas as pl
from jax.experimental.pallas import tpu as pltpu

F32 = jnp.float32
BF16 = jnp.bfloat16

D_MODEL = 1024
DEPTH = 2
CONV_WIDTH = 4
SSD_HEADS = 16
SSD_HEAD_DIM = 64
SSD_WIDTH = SSD_HEADS * SSD_HEAD_DIM
SSD_GROUPS = 2
SSD_HPG = SSD_HEADS // SSD_GROUPS
SSD_STATE = 64
SSD_GN = SSD_GROUPS * SSD_STATE
SSD_CONV_DIM = SSD_WIDTH + 2 * SSD_GN
LRU_WIDTH = 512
LRU_BLOCKS = 8
LRU_BLOCK_DIM = LRU_WIDTH // LRU_BLOCKS
LRU_C = 8.0
ATTN_Q_HEADS = 8
ATTN_KV_HEADS = 2
ATTN_GROUP = ATTN_Q_HEADS // ATTN_KV_HEADS
ATTN_HEAD_DIM = 64
ATTN_WIDTH = ATTN_Q_HEADS * ATTN_HEAD_DIM
KV_DIM = ATTN_KV_HEADS * ATTN_HEAD_DIM
WINDOW = 128
ATTN_SCALE = ATTN_HEAD_DIM ** -0.5
MIX_WIDTH = SSD_WIDTH + LRU_WIDTH + ATTN_WIDTH
N_EXPERTS = 8
TOP_K = 2
ALPHA = (2.0 * DEPTH) ** 0.25
LN_EPS = 1e-5
RMS_EPS = 1e-5

LANES = 128
SUBLANES = 8
VMEM_LIMIT_BYTES = 56 * 1024 * 1024

OFF_Z = 0
OFF_XBC = OFF_Z + SSD_WIDTH
OFF_LGATE = OFF_XBC + SSD_CONV_DIM
OFF_LX = OFF_LGATE + LRU_WIDTH
OFF_Q = OFF_LX + LRU_WIDTH
OFF_K = OFF_Q + ATTN_WIDTH
OFF_V = OFF_K + KV_DIM
OFF_DT = OFF_V + KV_DIM
PROJ_WIDTH = OFF_DT + LANES

CHUNK = 128
TOKEN_TILE = 512
PROJ_COL_TILE = PROJ_WIDTH // 3
FF_TILE = 1792
TOKEN_TILE_SHAPE = (SUBLANES, LANES)
assert SUBLANES * LANES == D_MODEL


def _cparams(*sem):
    return pltpu.CompilerParams(dimension_semantics=sem, vmem_limit_bytes=VMEM_LIMIT_BYTES)


def _dot(a, b):
    return jnp.dot(a.astype(BF16), b.astype(BF16), preferred_element_type=F32)


def _dot_nt(a, b):
    return lax.dot_general(a.astype(BF16), b.astype(BF16), (((1,), (1,)), ((), ())),
                           preferred_element_type=F32)


def _split3(x):
    hi = x.astype(BF16)
    r1 = x - hi.astype(F32)
    mid = r1.astype(BF16)
    lo = (r1 - mid.astype(F32)).astype(BF16)
    return hi, mid, lo


def _select_dot_rhs(x, sel):
    hi, mid, lo = _split3(x)
    d = lambda t: jnp.dot(t, sel, preferred_element_type=F32)
    return d(hi) + d(mid) + d(lo)


def _select_dot_lhs(sel, x):
    hi, mid, lo = _split3(x)
    d = lambda t: jnp.dot(sel, t, preferred_element_type=F32)
    return d(hi) + d(mid) + d(lo)


def _silu(x):
    return x * jax.nn.sigmoid(x)


def _layer_norm(h, g, b):
    mu = jnp.mean(h, axis=-1, keepdims=True)
    hc = h - mu
    var = jnp.mean(hc * hc, axis=-1, keepdims=True)
    return hc * lax.rsqrt(var + LN_EPS) * g + b


def _inproj_kernel(x_ref, w_ref, o_ref):
    o_ref[...] = jnp.dot(x_ref[...].astype(BF16), w_ref[...], preferred_element_type=F32)


def _in_proj(x, w):
    m = x.shape[0]
    tm = min(TOKEN_TILE, m)
    grid = (PROJ_WIDTH // PROJ_COL_TILE, m // tm)
    return pl.pallas_call(
        _inproj_kernel,
        grid=grid,
        in_specs=[pl.BlockSpec((tm, D_MODEL), lambda j, i: (i, 0)),
                  pl.BlockSpec((D_MODEL, PROJ_COL_TILE), lambda j, i: (0, j))],
        out_specs=pl.BlockSpec((tm, PROJ_COL_TILE), lambda j, i: (i, j)),
        out_shape=jax.ShapeDtypeStruct((m, PROJ_WIDTH), F32),
        compiler_params=_cparams("arbitrary", "arbitrary"),
        name="in_proj",
    )(x, w)


def _outproj_ln_kernel(y_ref, x_ref, w_ref, g_ref, b_ref, o_ref):
    mix = jnp.dot(y_ref[...].astype(BF16), w_ref[...], preferred_element_type=F32)
    o_ref[...] = _layer_norm(ALPHA * x_ref[...] + mix, g_ref[...], b_ref[...])


def _out_proj_ln(y_all, x, w, g, b):
    m = x.shape[0]
    tm = min(TOKEN_TILE, m)
    row = lambda i: (i, 0)
    fix = lambda i: (0, 0)
    return pl.pallas_call(
        _outproj_ln_kernel,
        grid=(m // tm,),
        in_specs=[pl.BlockSpec((tm, MIX_WIDTH), row), pl.BlockSpec((tm, D_MODEL), row),
                  pl.BlockSpec((MIX_WIDTH, D_MODEL), fix),
                  pl.BlockSpec((1, D_MODEL), fix), pl.BlockSpec((1, D_MODEL), fix)],
        out_specs=pl.BlockSpec((tm, D_MODEL), row),
        out_shape=jax.ShapeDtypeStruct((m, D_MODEL), F32),
        compiler_params=_cparams("arbitrary"),
        name="out_proj_ln",
    )(y_all, x, w, g, b)


def _swiglu_kernel(te_ref, tv_ref, x_ref, wg_ref, wu_ref, wd_ref, g_ref, b_ref, o_ref, acc_ref, xb_ref, *, fuse_ln):
    i = pl.program_id(0)
    f = pl.program_id(1)
    nf = pl.num_programs(1)
    live = tv_ref[i] > 0

    @pl.when(live & (f == 0))
    def _():
        xb_ref[...] = x_ref[...].reshape(xb_ref.shape).astype(BF16)

    @pl.when(live)
    def _():
        xb = xb_ref[...]
        hg = jnp.dot(xb, wg_ref[0], preferred_element_type=F32)
        hu = jnp.dot(xb, wu_ref[0], preferred_element_type=F32)
        part = jnp.dot((_silu(hg) * hu).astype(BF16), wd_ref[0], preferred_element_type=F32)

        @pl.when(f == 0)
        def _():
            acc_ref[...] = part

        @pl.when(f > 0)
        def _():
            acc_ref[...] += part

    @pl.when(live & (f == nf - 1))
    def _():
        if fuse_ln:
            o_ref[...] = _layer_norm(ALPHA * x_ref[...] + acc_ref[...], g_ref[...], b_ref[...])
        else:
            o_ref[...] = acc_ref[...].reshape(o_ref.shape)

    @pl.when(jnp.logical_not(live) & (f == nf - 1))
    def _():
        o_ref[...] = jnp.zeros_like(o_ref)


def _swiglu(x, tile_expert, tile_live, wg, wu, wd, g, b, *, tm, fuse_ln):
    m = x.shape[0]
    assert m % tm == 0 and tile_expert.shape[0] == m // tm and tile_live.shape[0] == m // tm
    row_block = (tm,) + x.shape[1:]
    row_map = (lambda i, f, te, tv: (i, 0)) if x.ndim == 2 else (lambda i, f, te, tv: (i, 0, 0))
    d_ff = wg.shape[2]
    ff = FF_TILE if d_ff % FF_TILE == 0 else d_ff // 2
    assert d_ff % ff == 0 and ff % LANES == 0
    nf = d_ff // ff
    fsel = lambda f, tv_i: f * tv_i + (nf - 1) * (1 - tv_i)
    fix = lambda i, f, te, tv: (0, 0)
    grid_spec = pltpu.PrefetchScalarGridSpec(
        num_scalar_prefetch=2,
        grid=(m // tm, nf),
        in_specs=[pl.BlockSpec(row_block, row_map),
                  pl.BlockSpec((1, D_MODEL, ff), lambda i, f, te, tv: (te[i], 0, fsel(f, tv[i]))),
                  pl.BlockSpec((1, D_MODEL, ff), lambda i, f, te, tv: (te[i], 0, fsel(f, tv[i]))),
                  pl.BlockSpec((1, ff, D_MODEL), lambda i, f, te, tv: (te[i], fsel(f, tv[i]), 0)),
                  pl.BlockSpec((1, D_MODEL), fix), pl.BlockSpec((1, D_MODEL), fix)],
        out_specs=pl.BlockSpec(row_block, row_map),
        scratch_shapes=[pltpu.VMEM((tm, D_MODEL), F32), pltpu.VMEM((tm, D_MODEL), BF16)],
    )
    return pl.pallas_call(
        functools.partial(_swiglu_kernel, fuse_ln=fuse_ln),
        grid_spec=grid_spec,
        out_shape=jax.ShapeDtypeStruct(x.shape, F32),
        compiler_params=_cparams("arbitrary", "arbitrary"),
        name="swiglu_ln" if fuse_ln else "swiglu_grouped",
    )(tile_expert, tile_live, x, wg, wu, wd, g, b)


def _router_kernel(x_ref, wr_ref, idx_ref, gate_ref, xt_ref):
    x = x_ref[...]
    xt_ref[...] = x.reshape(xt_ref.shape)
    logits = jnp.dot(x.astype(BF16), wr_ref[...], preferred_element_type=F32)
    lane = lax.broadcasted_iota(jnp.int32, logits.shape, 1)
    logits = jnp.where(lane < N_EXPERTS, logits, -jnp.inf)
    v1 = jnp.max(logits, axis=-1, keepdims=True)
    i1 = jnp.min(jnp.where(logits == v1, lane, LANES), axis=-1, keepdims=True)
    rest = jnp.where(lane == i1, -jnp.inf, logits)
    v2 = jnp.max(rest, axis=-1, keepdims=True)
    i2 = jnp.min(jnp.where(rest == v2, lane, LANES), axis=-1, keepdims=True)
    e2 = jnp.exp(v2 - v1)
    den = 1.0 + e2
    idx_ref[...] = jnp.where(lane == 0, i1, jnp.where(lane == 1, i2, 0))
    gate_ref[...] = jnp.where(lane == 0, 1.0 / den, jnp.where(lane == 1, e2 / den, 0.0))


def _router(x, wr):
    m = x.shape[0]
    tm = min(TOKEN_TILE, m)
    row = lambda i: (i, 0)
    return pl.pallas_call(
        _router_kernel,
        grid=(m // tm,),
        in_specs=[pl.BlockSpec((tm, D_MODEL), row), pl.BlockSpec((D_MODEL, LANES), lambda i: (0, 0))],
        out_specs=[pl.BlockSpec((tm, LANES), row), pl.BlockSpec((tm, LANES), row),
                   pl.BlockSpec((tm,) + TOKEN_TILE_SHAPE, lambda i: (i, 0, 0))],
        out_shape=[jax.ShapeDtypeStruct((m, LANES), jnp.int32), jax.ShapeDtypeStruct((m, LANES), F32),
                   jax.ShapeDtypeStruct((m,) + TOKEN_TILE_SHAPE, F32)],
        compiler_params=_cparams("arbitrary"),
        name="router_top2",
    )(x, wr)


def _row_gather_kernel(idx_ref, src_ref, dst_ref, sem, *, rows):
    def start(r, carry):
        pltpu.make_async_copy(src_ref.at[pl.ds(idx_ref[0, 0, r], 1)], dst_ref.at[pl.ds(r, 1)], sem).start()
        return carry

    lax.fori_loop(0, rows, start, 0, unroll=8)
    pltpu.make_async_copy(src_ref.at[pl.ds(0, rows)], dst_ref, sem).wait()


def _row_gather(src, idx, rows):
    n_rows = idx.shape[0]
    assert n_rows % rows == 0
    steps = n_rows // rows
    return pl.pallas_call(
        functools.partial(_row_gather_kernel, rows=rows),
        grid=(steps,),
        in_specs=[pl.BlockSpec((1, 1, rows), lambda i: (i, 0, 0), memory_space=pltpu.SMEM),
                  pl.BlockSpec(memory_space=pl.ANY)],
        out_specs=pl.BlockSpec((rows,) + src.shape[1:], lambda i: (i, 0, 0)),
        out_shape=jax.ShapeDtypeStruct((n_rows,) + src.shape[1:], src.dtype),
        scratch_shapes=[pltpu.SemaphoreType.DMA],
        compiler_params=_cparams("arbitrary"),
        name="row_gather",
    )(idx.reshape(steps, 1, rows), src)


def _combine_ln_kernel(x_ref, y_ref, gate_ref, g_ref, b_ref, o_ref):
    gate = gate_ref[...]
    x = x_ref[...]
    f = gate[:, 0:1] * y_ref[:, 0].reshape(x.shape) + gate[:, 1:2] * y_ref[:, 1].reshape(x.shape)
    o_ref[...] = _layer_norm(ALPHA * x + f, g_ref[...], b_ref[...])


def _combine_ln(x, y_pairs, gates, g, b):
    m = x.shape[0]
    tm = min(TOKEN_TILE, m)
    row = lambda i: (i, 0)
    fix = lambda i: (0, 0)
    return pl.pallas_call(
        _combine_ln_kernel,
        grid=(m // tm,),
        in_specs=[pl.BlockSpec((tm, D_MODEL), row),
                  pl.BlockSpec((tm, TOP_K) + TOKEN_TILE_SHAPE, lambda i: (i, 0, 0, 0)),
                  pl.BlockSpec((tm, LANES), row),
                  pl.BlockSpec((1, D_MODEL), fix), pl.BlockSpec((1, D_MODEL), fix)],
        out_specs=pl.BlockSpec((tm, D_MODEL), row),
        out_shape=jax.ShapeDtypeStruct((m, D_MODEL), F32),
        compiler_params=_cparams("arbitrary"),
        name="moe_combine_ln",
    )(x, y_pairs, gates, g, b)


def _moe_ffn_ln(x, wr, wg, wu, wd, g, b):
    m = x.shape[0]
    tm = min(TOKEN_TILE, m)
    top_i, gates, x_tiles = _router(x, wr)
    e_flat = top_i[:, :TOP_K].reshape(-1)
    onehot = (e_flat[:, None] == jnp.arange(N_EXPERTS, dtype=jnp.int32)[None, :]).astype(jnp.int32)
    blk = min(LANES, TOP_K * m)
    inner = jnp.cumsum(onehot.reshape(-1, blk, N_EXPERTS), axis=1)
    block_tot = inner[:, blk - 1, :]
    running = (inner + (jnp.cumsum(block_tot, axis=0) - block_tot)[:, None, :]).reshape(-1, N_EXPERTS)
    rank = jnp.sum(running * onehot, axis=1) - 1
    counts = jnp.sum(onehot, axis=0)
    padded = ((counts + tm - 1) // tm) * tm
    ends = jnp.cumsum(padded)
    starts = ends - padded
    pos = starts[e_flat] + rank
    n_pad = TOP_K * m + N_EXPERTS * tm
    n_tiles = n_pad // tm
    row_token = jnp.zeros((n_pad,), jnp.int32).at[pos].set(jnp.arange(TOP_K * m, dtype=jnp.int32) // TOP_K)
    tile_start = jnp.arange(n_tiles, dtype=jnp.int32) * tm
    tile_expert = jnp.minimum(jnp.sum((tile_start[:, None] >= ends[None, :]).astype(jnp.int32), axis=1),
                              N_EXPERTS - 1).astype(jnp.int32)
    tile_live = (tile_start < ends[N_EXPERTS - 1]).astype(jnp.int32)
    xs = _row_gather(x_tiles, row_token, tm)
    ys = _swiglu(xs, tile_expert, tile_live, wg, wu, wd, g, b, tm=tm, fuse_ln=False)
    y_pairs = _row_gather(ys, pos.astype(jnp.int32), tm).reshape((m, TOP_K) + TOKEN_TILE_SHAPE)
    return _combine_ln(x, y_pairs, gates, g, b)


def _shift_rows(x, tail, sh):
    if sh == 0:
        return x
    row8 = lax.broadcasted_iota(jnp.int32, (SUBLANES, 1), 0)
    xr = pltpu.roll(x, sh, 0)
    head = jnp.where(row8 < sh, pltpu.roll(tail, sh, 0), xr[0:SUBLANES])
    return jnp.concatenate([head, xr[SUBLANES:]], axis=0)


def _causal_conv(x, tail, w_ref, b_ref):
    acc = b_ref[...] + _shift_rows(x, tail, CONV_WIDTH - 1) * w_ref[0:1, :]
    for j in range(1, CONV_WIDTH):
        acc = acc + _shift_rows(x, tail, CONV_WIDTH - 1 - j) * w_ref[j:j + 1, :]
    return acc


def _lru_gates(xl, wax_ref, bax_ref, lam_ref):
    gates = jnp.dot(xl.astype(BF16), wax_ref[...], preferred_element_type=F32) + bax_ref[...]
    r = jax.nn.sigmoid(gates[:, :LRU_WIDTH])
    i = jax.nn.sigmoid(gates[:, LRU_WIDTH:])
    log_a = -LRU_C * r * jax.nn.softplus(-lam_ref[...])
    a = jnp.exp(log_a)
    th = jnp.tanh(log_a)
    u = jnp.sqrt(-2.0 * th / (1.0 - th)) * (i * xl)
    return a, u


def _sink_softmax(s, sink):
    m = jnp.maximum(jnp.max(s, axis=-1, keepdims=True), sink)
    e = jnp.exp(s - m)
    den = jnp.sum(e, axis=-1, keepdims=True) + jnp.exp(sink - m)
    return e, den


def _mixer_prompt_kernel(sink_ref, proj_ref, cw_ref, cb_ref, dtb_ref, alog_ref, drow_ref, ng_ref, rexp_ref,
                         gmask_ref, lcw_ref, lcb_ref, wax_ref, bax_ref, lam_ref,
                         y_ref, s_out_ref, hl_out_ref,
                         xtail, ltail, s_scr, hl_scr, kprev, vprev):
    c = pl.program_id(1)
    nc = pl.num_programs(1)
    t = CHUNK

    @pl.when(c == 0)
    def _():
        xtail[...] = jnp.zeros_like(xtail)
        ltail[...] = jnp.zeros_like(ltail)
        s_scr[...] = jnp.zeros_like(s_scr)
        hl_scr[...] = jnp.zeros_like(hl_scr)
        kprev[...] = jnp.zeros_like(kprev)
        vprev[...] = jnp.zeros_like(vprev)

    row = lax.broadcasted_iota(jnp.int32, (t, 1), 0)
    col = lax.broadcasted_iota(jnp.int32, (1, t), 1)
    lane = lax.broadcasted_iota(jnp.int32, (1, LANES), 1)
    causal = row >= col

    z = proj_ref[:, OFF_Z:OFF_Z + SSD_WIDTH]
    xbc_raw = proj_ref[:, OFF_XBC:OFF_XBC + SSD_CONV_DIM]
    xbc = _silu(_causal_conv(xbc_raw, xtail[...], cw_ref, cb_ref))
    xtail[...] = xbc_raw[t - SUBLANES:t]
    xs = xbc[:, :SSD_WIDTH]
    bm = xbc[:, SSD_WIDTH:SSD_WIDTH + SSD_GN]
    cm = xbc[:, SSD_WIDTH + SSD_GN:]
    dt = jax.nn.softplus(proj_ref[:, OFF_DT:OFF_DT + LANES] + dtb_ref[...])
    da = dt * (-jnp.exp(alog_ref[...]))
    acum = _select_dot_lhs(causal.astype(BF16), da)
    acum_t = acum.T
    rexp = rexp_ref[...]
    acum_e = _select_dot_rhs(acum, rexp)
    dt_e = _select_dot_rhs(dt, rexp)
    xdt = xs * dt_e
    exp_e = jnp.exp(acum_e)
    last_e = acum_e[t - 1:t, :]
    xw = xdt * jnp.exp(last_e - acum_e)
    bmb = bm.astype(BF16)

    pairs = []
    for g in range(SSD_GROUPS):
        gsel = (lane >= g * SSD_STATE) & (lane < (g + 1) * SSD_STATE)
        cb = _dot_nt(jnp.where(gsel, cm, 0.0), bmb)
        for pc in range(SSD_HPG // 2):
            p2 = g * (SSD_HPG // 2) + pc
            ms = []
            for j in (2 * p2, 2 * p2 + 1):
                seg = acum[:, j:j + 1] - acum_t[j:j + 1, :]
                dec = jnp.exp(jnp.where(causal, seg, -jnp.inf))
                ms.append((cb * dec).astype(BF16))
            lhs = jnp.concatenate(ms, axis=1)
            xp = xdt[:, LANES * p2:LANES * (p2 + 1)]
            rhs = jnp.concatenate([jnp.where(lane < SSD_HEAD_DIM, xp, 0.0),
                                   jnp.where(lane >= SSD_HEAD_DIM, xp, 0.0)], axis=0).astype(BF16)
            pairs.append(jnp.dot(lhs, rhs, preferred_element_type=F32))
    y_diag = jnp.concatenate(pairs, axis=1)

    s_old = s_scr[...]
    y_off = _dot(cm, s_old) * exp_e
    btx = _dot(bm.T, xw)
    s_scr[...] = s_old * exp_e[t - 1:t, :] + jnp.where(gmask_ref[...] > 0.0, btx, 0.0)

    y = (y_diag + y_off + drow_ref[...] * xs) * _silu(z)
    half = SSD_WIDTH // SSD_GROUPS
    normed = []
    for g in range(SSD_GROUPS):
        seg = y[:, g * half:(g + 1) * half]
        normed.append(seg * lax.rsqrt(jnp.mean(seg * seg, axis=-1, keepdims=True) + RMS_EPS))
    y_ssd = jnp.concatenate(normed, axis=1) * ng_ref[...]

    lx_raw = proj_ref[:, OFF_LX:OFF_LX + LRU_WIDTH]
    xl = _causal_conv(lx_raw, ltail[...], lcw_ref, lcb_ref)
    ltail[...] = lx_raw[t - SUBLANES:t]
    a_s, u_s = _lru_gates(xl, wax_ref, bax_ref, lam_ref)
    d = 1
    while d < t:
        a_sh = jnp.where(row >= d, pltpu.roll(a_s, d, 0), 1.0)
        u_sh = jnp.where(row >= d, pltpu.roll(u_s, d, 0), 0.0)
        u_s = a_s * u_sh + u_s
        a_s = a_s * a_sh
        d *= 2
    h_seq = a_s * hl_scr[...] + u_s
    hl_scr[...] = h_seq[t - 1:t, :]
    y_lru = h_seq * jax.nn.gelu(proj_ref[:, OFF_LGATE:OFF_LGATE + LRU_WIDTH])

    k = proj_ref[:, OFF_K:OFF_K + KV_DIM]
    v = proj_ref[:, OFF_V:OFF_V + KV_DIM]
    kc = jnp.concatenate([kprev[...], k], axis=0).astype(BF16)
    vc = jnp.concatenate([vprev[...], v], axis=0).astype(BF16)
    kprev[...] = k
    vprev[...] = v
    si = lax.broadcasted_iota(jnp.int32, (1, 2 * t), 1)
    first_key = jnp.where(c > 0, 0, t)
    valid = (si > row) & (si <= row + t) & (si >= first_key)
    blocks = []
    for g in range(ATTN_GROUP):
        qb = proj_ref[:, OFF_Q + LANES * g:OFF_Q + LANES * (g + 1)]
        halves = []
        for kv in range(ATTN_KV_HEADS):
            hsel = (lane >= kv * ATTN_HEAD_DIM) & (lane < (kv + 1) * ATTN_HEAD_DIM)
            s = _dot_nt(jnp.where(hsel, qb, 0.0), kc) * ATTN_SCALE
            s = jnp.where(valid, s, -jnp.inf)
            e, den = _sink_softmax(s, sink_ref[kv * ATTN_GROUP + g])
            halves.append(jnp.dot((e / den).astype(BF16), vc, preferred_element_type=F32))
        blocks.append(jnp.where(lane < ATTN_HEAD_DIM, halves[0], halves[1]))
    y_attn = jnp.concatenate(blocks, axis=1)

    y_ref[...] = jnp.concatenate([y_ssd, y_lru, y_attn], axis=1).astype(y_ref.dtype)

    @pl.when(c == nc - 1)
    def _():
        s_out_ref[0] = s_scr[...]
        hl_out_ref[0] = hl_scr[...]


def _mixer_prompt(proj, lw, bsz, seq):
    nc = seq // CHUNK
    fix = lambda b, c: (0, 0)
    vec = lambda n: pl.BlockSpec((1, n), fix)
    in_specs = [
        pl.BlockSpec(memory_space=pltpu.SMEM),
        pl.BlockSpec((CHUNK, PROJ_WIDTH), lambda b, c: (b * nc + c, 0)),
        pl.BlockSpec((CONV_WIDTH, SSD_CONV_DIM), fix), vec(SSD_CONV_DIM),
        vec(LANES), vec(LANES), vec(SSD_WIDTH), vec(SSD_WIDTH),
        pl.BlockSpec((LANES, SSD_WIDTH), fix), pl.BlockSpec((SSD_GN, SSD_WIDTH), fix),
        pl.BlockSpec((CONV_WIDTH, LRU_WIDTH), fix), vec(LRU_WIDTH),
        pl.BlockSpec((LRU_WIDTH, 2 * LRU_WIDTH), fix), vec(2 * LRU_WIDTH), vec(LRU_WIDTH),
    ]
    out_specs = [
        pl.BlockSpec((CHUNK, MIX_WIDTH), lambda b, c: (b * nc + c, 0)),
        pl.BlockSpec((1, SSD_GN, SSD_WIDTH), lambda b, c: (b, 0, 0)),
        pl.BlockSpec((1, 1, LRU_WIDTH), lambda b, c: (b, 0, 0)),
    ]
    out_shape = [
        jax.ShapeDtypeStruct((bsz * seq, MIX_WIDTH), BF16),
        jax.ShapeDtypeStruct((bsz, SSD_GN, SSD_WIDTH), F32),
        jax.ShapeDtypeStruct((bsz, 1, LRU_WIDTH), F32),
    ]
    scratch = [
        pltpu.VMEM((SUBLANES, SSD_CONV_DIM), F32), pltpu.VMEM((SUBLANES, LRU_WIDTH), F32),
        pltpu.VMEM((SSD_GN, SSD_WIDTH), F32), pltpu.VMEM((1, LRU_WIDTH), F32),
        pltpu.VMEM((CHUNK, KV_DIM), F32), pltpu.VMEM((CHUNK, KV_DIM), F32),
    ]
    return pl.pallas_call(
        _mixer_prompt_kernel,
        grid=(bsz, nc),
        in_specs=in_specs,
        out_specs=out_specs,
        out_shape=out_shape,
        scratch_shapes=scratch,
        compiler_params=_cparams("arbitrary", "arbitrary"),
        name="mixer_prompt",
    )(lw['sinks'], proj, lw['ssd_conv_w'], lw['ssd_conv_b'], lw['dt_bias'], lw['a_log'], lw['d_row'],
      lw['norm_g'], lw['head_expand'], lw['group_mask'], lw['lru_conv_w'], lw['lru_conv_b'],
      lw['lru_wax'], lw['lru_bax'], lw['lru_lambda'])


DEC_ROWS = SUBLANES


def _conv_step(x, buf_ref, w_ref, b_ref):
    acc = b_ref[...] + buf_ref[0] * w_ref[0:1, :]
    for j in range(1, CONV_WIDTH - 1):
        acc = acc + buf_ref[j] * w_ref[j:j + 1, :]
    return acc + x * w_ref[CONV_WIDTH - 1:CONV_WIDTH, :]


def _mixer_decode_kernel(sink_ref, proj_ref, sbuf_ref, lbuf_ref, hl_ref, kc_ref, vc_ref, h_ref,
                         cw_ref, cb_ref, dtb_ref, alog_ref, drow_ref, ng_ref, rexp_ref,
                         lcw_ref, lcb_ref, wax_ref, bax_ref, lam_ref,
                         y_ref, sbuf_out, lbuf_out, hl_out, kc_out, vc_out, h_out):
    nb = DEC_ROWS
    row8 = lax.broadcasted_iota(jnp.int32, (nb, 1), 0)
    lane = lax.broadcasted_iota(jnp.int32, (1, LANES), 1)
    wrow = lax.broadcasted_iota(jnp.int32, (WINDOW, 1), 0)

    z = proj_ref[:, OFF_Z:OFF_Z + SSD_WIDTH]
    xbc_raw = proj_ref[:, OFF_XBC:OFF_XBC + SSD_CONV_DIM]
    xbc = _silu(_conv_step(xbc_raw, sbuf_ref, cw_ref, cb_ref))
    for j in range(CONV_WIDTH - 2):
        sbuf_out[j] = sbuf_ref[j + 1]
    sbuf_out[CONV_WIDTH - 2] = xbc_raw
    xs = xbc[:, :SSD_WIDTH]
    bm = xbc[:, SSD_WIDTH:SSD_WIDTH + SSD_GN]
    cm = xbc[:, SSD_WIDTH + SSD_GN:]
    dt = jax.nn.softplus(proj_ref[:, OFF_DT:OFF_DT + LANES] + dtb_ref[...])
    da = dt * (-jnp.exp(alog_ref[...]))
    rexp = rexp_ref[...]
    decay_e = jnp.exp(_select_dot_rhs(da, rexp))
    xdt = xs * _select_dot_rhs(dt, rexp)
    cols = jnp.concatenate([xdt, decay_e, jnp.zeros((LANES - 2 * nb, SSD_WIDTH), F32)], axis=0)
    cols_t = [cols[:, LANES * p2:LANES * (p2 + 1)].T for p2 in range(SSD_HEADS // 2)]
    bm_g = [bm[:, :SSD_STATE], pltpu.roll(bm, SSD_STATE, 1)[:, :SSD_STATE]]
    cm_g = [cm[:, :SSD_STATE], pltpu.roll(cm, SSD_STATE, 1)[:, :SSD_STATE]]
    half = SSD_WIDTH // SSD_GROUPS
    y_off = [jnp.zeros((nb, half), F32) for _ in range(SSD_GROUPS)]
    for bb in range(nb):
        for g in range(SSD_GROUPS):
            h_g = h_ref[bb, g * SSD_HPG:(g + 1) * SSD_HPG].reshape(half, SSD_STATE)
            y_off[g] = jnp.where(row8 == bb, _dot_nt(cm_g[g], h_g), y_off[g])
        for p2 in range(SSD_HEADS // 2):
            g = p2 // (SSD_HPG // 2)
            h_pair = h_ref[bb, 2 * p2:2 * p2 + 2].reshape(2 * SSD_HEAD_DIM, SSD_STATE)
            xcol = cols_t[p2][:, bb:bb + 1]
            dcol = cols_t[p2][:, nb + bb:nb + bb + 1]
            h_new = h_pair * dcol + xcol * bm_g[g][bb:bb + 1, :]
            h_out[bb, 2 * p2:2 * p2 + 2] = h_new.reshape(2, SSD_HEAD_DIM, SSD_STATE)
    cbv = cm * bm
    ys = []
    for g in range(SSD_GROUPS):
        gsel = (lane >= g * SSD_STATE) & (lane < (g + 1) * SSD_STATE)
        cb_g = jnp.sum(jnp.where(gsel, cbv, 0.0), axis=-1, keepdims=True)
        sl = slice(g * half, (g + 1) * half)
        ys.append(cb_g * xdt[:, sl] + y_off[g] * decay_e[:, sl])
    y = (jnp.concatenate(ys, axis=1) + drow_ref[...] * xs) * _silu(z)
    normed = []
    for g in range(SSD_GROUPS):
        seg = y[:, g * half:(g + 1) * half]
        normed.append(seg * lax.rsqrt(jnp.mean(seg * seg, axis=-1, keepdims=True) + RMS_EPS))
    y_ssd = jnp.concatenate(normed, axis=1) * ng_ref[...]

    lx_raw = proj_ref[:, OFF_LX:OFF_LX + LRU_WIDTH]
    xl = _conv_step(lx_raw, lbuf_ref, lcw_ref, lcb_ref)
    for j in range(CONV_WIDTH - 2):
        lbuf_out[j] = lbuf_ref[j + 1]
    lbuf_out[CONV_WIDTH - 2] = lx_raw
    a_s, u_s = _lru_gates(xl, wax_ref, bax_ref, lam_ref)
    h_lru = a_s * hl_ref[...] + u_s
    hl_out[...] = h_lru
    y_lru = h_lru * jax.nn.gelu(proj_ref[:, OFF_LGATE:OFF_LGATE + LRU_WIDTH])

    k_new = proj_ref[:, OFF_K:OFF_K + KV_DIM]
    v_new = proj_ref[:, OFF_V:OFF_V + KV_DIM]
    head_row_g = row8 // ATTN_KV_HEADS
    head_row_kv = row8 % ATTN_KV_HEADS
    lane_kv = lane // ATTN_HEAD_DIM
    sink_col = jnp.zeros((nb, 1), F32)
    for r in range(ATTN_Q_HEADS):
        sink_col = jnp.where(row8 == r, sink_ref[(r % ATTN_KV_HEADS) * ATTN_GROUP + r // ATTN_KV_HEADS], sink_col)
    y_blocks = [jnp.zeros((nb, LANES), F32) for _ in range(ATTN_GROUP)]
    for bb in range(nb):
        q_rows = jnp.zeros((ATTN_Q_HEADS, LANES), F32)
        for g in range(ATTN_GROUP):
            qb = proj_ref[bb:bb + 1, OFF_Q + LANES * g:OFF_Q + LANES * (g + 1)]
            q_rows = jnp.where((head_row_g == g) & (head_row_kv == lane_kv), qb, q_rows)
        kb = kc_ref[bb]
        vb = vc_ref[bb]
        kn = k_new[bb:bb + 1, :]
        vn = v_new[bb:bb + 1, :]
        s = _dot_nt(q_rows, kb) * ATTN_SCALE
        s = jnp.where(lane >= 1, s, -jnp.inf)
        s_new = jnp.sum(q_rows * kn, axis=-1, keepdims=True) * ATTN_SCALE
        m = jnp.maximum(jnp.maximum(jnp.max(s, axis=-1, keepdims=True), s_new), sink_col)
        e = jnp.exp(s - m)
        e_new = jnp.exp(s_new - m)
        den = jnp.sum(e, axis=-1, keepdims=True) + e_new + jnp.exp(sink_col - m)
        o = _dot(e / den, vb) + (e_new / den) * vn
        for g in range(ATTN_GROUP):
            blk = jnp.where(lane < ATTN_HEAD_DIM, o[2 * g:2 * g + 1, :], o[2 * g + 1:2 * g + 2, :])
            y_blocks[g] = jnp.where(row8 == bb, blk, y_blocks[g])
        kc_out[bb] = jnp.where(wrow == WINDOW - 1, kn, pltpu.roll(kb, WINDOW - 1, 0))
        vc_out[bb] = jnp.where(wrow == WINDOW - 1, vn, pltpu.roll(vb, WINDOW - 1, 0))
    y_attn = jnp.concatenate(y_blocks, axis=1)

    y_ref[...] = jnp.concatenate([y_ssd, y_lru, y_attn], axis=1)


def _mixer_decode(proj, lw, sbuf, lbuf, hl, kc, vc, h):
    bsz = proj.shape[0]
    nb = DEC_ROWS
    fix = lambda i: (0, 0)
    vec = lambda n: pl.BlockSpec((1, n), fix)
    rows = lambda n: pl.BlockSpec((nb, n), lambda i: (i, 0))
    buf = lambda n: pl.BlockSpec((CONV_WIDTH - 1, nb, n), lambda i: (0, i, 0))
    cache = pl.BlockSpec((nb, WINDOW, KV_DIM), lambda i: (i, 0, 0))
    state = pl.BlockSpec((nb, SSD_HEADS, SSD_HEAD_DIM, SSD_STATE), lambda i: (i, 0, 0, 0))
    in_specs = [
        pl.BlockSpec(memory_space=pltpu.SMEM),
        rows(PROJ_WIDTH), buf(SSD_CONV_DIM), buf(LRU_WIDTH), rows(LRU_WIDTH), cache, cache, state,
        pl.BlockSpec((CONV_WIDTH, SSD_CONV_DIM), fix), vec(SSD_CONV_DIM),
        vec(LANES), vec(LANES), vec(SSD_WIDTH), vec(SSD_WIDTH),
        pl.BlockSpec((LANES, SSD_WIDTH), fix),
        pl.BlockSpec((CONV_WIDTH, LRU_WIDTH), fix), vec(LRU_WIDTH),
        pl.BlockSpec((LRU_WIDTH, 2 * LRU_WIDTH), fix), vec(2 * LRU_WIDTH), vec(LRU_WIDTH),
    ]
    out_specs = [rows(MIX_WIDTH), buf(SSD_CONV_DIM), buf(LRU_WIDTH), rows(LRU_WIDTH), cache, cache, state]
    out_shape = [
        jax.ShapeDtypeStruct((bsz, MIX_WIDTH), F32),
        jax.ShapeDtypeStruct(sbuf.shape, F32), jax.ShapeDtypeStruct(lbuf.shape, F32),
        jax.ShapeDtypeStruct(hl.shape, F32), jax.ShapeDtypeStruct(kc.shape, F32),
        jax.ShapeDtypeStruct(vc.shape, F32), jax.ShapeDtypeStruct(h.shape, F32),
    ]
    return pl.pallas_call(
        _mixer_decode_kernel,
        grid=(bsz // nb,),
        in_specs=in_specs,
        out_specs=out_specs,
        out_shape=out_shape,
        compiler_params=_cparams("arbitrary"),
        name="mixer_decode",
    )(lw['sinks'], proj, sbuf, lbuf, hl, kc, vc, h,
      lw['ssd_conv_w'], lw['ssd_conv_b'], lw['dt_bias'], lw['a_log'], lw['d_row'], lw['norm_g'],
      lw['head_expand'], lw['lru_conv_w'], lw['lru_conv_b'], lw['lru_wax'], lw['lru_bax'], lw['lru_lambda'])


def _q_perm():
    g, kv, d = jnp.meshgrid(jnp.arange(ATTN_GROUP), jnp.arange(ATTN_KV_HEADS), jnp.arange(ATTN_HEAD_DIM),
                            indexing='ij')
    return (kv * ATTN_GROUP * ATTN_HEAD_DIM + g * ATTN_HEAD_DIM + d).reshape(-1)


def _block_diag(w):
    eye = jnp.eye(LRU_BLOCKS, dtype=w.dtype)
    return jnp.einsum('kij,kl->kilj', w, eye).reshape(LRU_WIDTH, LRU_WIDTH)


def _prep_layer(l, p):
    w_in = p['w_in'][l]
    o = 0
    segs = {}
    for name, size in (('z', SSD_WIDTH), ('xbc', SSD_CONV_DIM), ('dt', SSD_HEADS), ('lgate', LRU_WIDTH),
                       ('lx', LRU_WIDTH), ('q', ATTN_WIDTH), ('k', KV_DIM), ('v', KV_DIM)):
        segs[name] = w_in[:, o:o + size]
        o += size
    qp = _q_perm()
    w_in_p = jnp.concatenate(
        [segs['z'], segs['xbc'], segs['lgate'], segs['lx'], segs['q'][:, qp], segs['k'], segs['v'],
         jnp.pad(segs['dt'], ((0, 0), (0, LANES - SSD_HEADS)))], axis=1).astype(BF16)
    w_out = p['w_out'][l]
    attn_rows = w_out[SSD_WIDTH + LRU_WIDTH:]
    w_out_p = jnp.concatenate([w_out[:SSD_WIDTH + LRU_WIDTH], attn_rows[qp]], axis=0).astype(BF16)
    pad_h = lambda v: jnp.pad(v, (0, LANES - SSD_HEADS)).reshape(1, LANES)
    head_of_lane = jnp.arange(SSD_WIDTH) // SSD_HEAD_DIM
    head_expand = (jnp.arange(LANES)[:, None] == head_of_lane[None, :]).astype(BF16)
    group_mask = ((jnp.arange(SSD_GN)[:, None] // SSD_STATE) == (head_of_lane[None, :] // SSD_HPG)).astype(F32)
    return dict(
        w_in=w_in_p, w_out=w_out_p,
        ssd_conv_w=p['ssd_conv_w'][l], ssd_conv_b=p['ssd_conv_b'][l].reshape(1, -1),
        dt_bias=pad_h(p['ssd_dt_bias'][l]), a_log=pad_h(p['ssd_a_log'][l]),
        d_row=jnp.repeat(p['ssd_d'][l], SSD_HEAD_DIM).reshape(1, -1),
        norm_g=p['ssd_norm_g'][l].reshape(1, -1),
        head_expand=head_expand, group_mask=group_mask,
        lru_conv_w=p['lru_conv_w'][l], lru_conv_b=p['lru_conv_b'][l].reshape(1, -1),
        lru_wax=jnp.concatenate([_block_diag(p['lru_wa'][l]), _block_diag(p['lru_wx'][l])], axis=1).astype(BF16),
        lru_bax=jnp.concatenate([p['lru_ba'][l].reshape(-1), p['lru_bx'][l].reshape(-1)]).reshape(1, -1),
        lru_lambda=p['lru_lambda'][l].reshape(1, -1),
        sinks=p['attn_sinks'][l],
        channel=_prep_channel(l, p),
        ln1_g=p['ln1_g'][l].reshape(1, -1), ln1_b=p['ln1_b'][l].reshape(1, -1),
        ln2_g=p['ln2_g'][l].reshape(1, -1), ln2_b=p['ln2_b'][l].reshape(1, -1),
    )


def _prep_channel(l, p):
    i = l // 2
    if l % 2 == 0:
        return dict(wg=p['ffn_w_gate'][i][None].astype(BF16), wu=p['ffn_w_up'][i][None].astype(BF16),
                    wd=p['ffn_w_down'][i][None].astype(BF16))
    return dict(wr=jnp.pad(p['moe_router'][i], ((0, 0), (0, LANES - N_EXPERTS))).astype(BF16),
                wg=p['moe_w_gate'][i].astype(BF16), wu=p['moe_w_up'][i].astype(BF16),
                wd=p['moe_w_down'][i].astype(BF16))


def _channel_mixer(l, x, lw):
    cw = lw['channel']
    if l % 2 == 0:
        tm = min(TOKEN_TILE, x.shape[0])
        n_tiles = x.shape[0] // tm
        return _swiglu(x, jnp.zeros((n_tiles,), jnp.int32), jnp.ones((n_tiles,), jnp.int32),
                       cw['wg'], cw['wu'], cw['wd'], lw['ln2_g'], lw['ln2_b'], tm=tm, fuse_ln=True)
    return _moe_ffn_ln(x, cw['wr'], cw['wg'], cw['wu'], cw['wd'], lw['ln2_g'], lw['ln2_b'])


def _ssd_state_from_scratch_layout(s):
    bsz = s.shape[0]
    s6 = s.reshape(bsz, SSD_GROUPS, SSD_STATE, SSD_GROUPS, SSD_HPG, SSD_HEAD_DIM)
    diag = jnp.stack([s6[:, g, :, g] for g in range(SSD_GROUPS)], axis=1)
    return jnp.transpose(diag, (0, 1, 3, 4, 2)).reshape(bsz, SSD_HEADS, SSD_HEAD_DIM, SSD_STATE)


def _prompt_trunk(x_prompt, lws, p):
    bsz, seq, _ = x_prompt.shape
    x = x_prompt.reshape(bsz * seq, D_MODEL)
    states = [[] for _ in range(6)]
    for l in range(DEPTH):
        lw = lws[l]
        proj = _in_proj(x, lw['w_in'])
        y_all, s_fin, hl_fin = _mixer_prompt(proj, lw, bsz, seq)
        proj3 = proj.reshape(bsz, seq, PROJ_WIDTH)
        states[0].append(_ssd_state_from_scratch_layout(s_fin))
        states[1].append(proj3[:, seq - (CONV_WIDTH - 1):, OFF_XBC:OFF_XBC + SSD_CONV_DIM])
        states[2].append(hl_fin.reshape(bsz, LRU_WIDTH))
        states[3].append(proj3[:, seq - (CONV_WIDTH - 1):, OFF_LX:OFF_LX + LRU_WIDTH])
        states[4].append(proj3[:, seq - WINDOW:, OFF_K:OFF_K + KV_DIM].reshape(bsz, WINDOW, ATTN_KV_HEADS, ATTN_HEAD_DIM))
        states[5].append(proj3[:, seq - WINDOW:, OFF_V:OFF_V + KV_DIM].reshape(bsz, WINDOW, ATTN_KV_HEADS, ATTN_HEAD_DIM))
        x = _out_proj_ln(y_all, x, lw['w_out'], lw['ln1_g'], lw['ln1_b'])
        x = _channel_mixer(l, x, lw)
    return x.reshape(bsz, seq, D_MODEL), [jnp.stack(s, axis=0) for s in states]


def _sample_trunk(x_sample, lws, p, state_ssd, state_ssd_conv, state_lru, state_lru_conv, cache_k, cache_v):
    bsz = x_sample.shape[0]
    x = x_sample.reshape(bsz, D_MODEL)
    states = [[] for _ in range(6)]
    for l in range(DEPTH):
        lw = lws[l]
        proj = _in_proj(x, lw['w_in'])
        y_all, sbuf, lbuf, hl, kc, vc, h = _mixer_decode(
            proj, lw, jnp.transpose(state_ssd_conv[l], (1, 0, 2)), jnp.transpose(state_lru_conv[l], (1, 0, 2)),
            state_lru[l], cache_k[l].reshape(bsz, WINDOW, KV_DIM), cache_v[l].reshape(bsz, WINDOW, KV_DIM),
            state_ssd[l])
        states[0].append(h)
        states[1].append(jnp.transpose(sbuf, (1, 0, 2)))
        states[2].append(hl)
        states[3].append(jnp.transpose(lbuf, (1, 0, 2)))
        states[4].append(kc.reshape(bsz, WINDOW, ATTN_KV_HEADS, ATTN_HEAD_DIM))
        states[5].append(vc.reshape(bsz, WINDOW, ATTN_KV_HEADS, ATTN_HEAD_DIM))
        x = _out_proj_ln(y_all, x, lw['w_out'], lw['ln1_g'], lw['ln1_b'])
        x = _channel_mixer(l, x, lw)
    return x.reshape(bsz, 1, D_MODEL), [jnp.stack(s, axis=0) for s in states]


def kernel(x_prompt, x_sample, state_ssd, state_ssd_conv, state_lru, state_lru_conv, cache_swa_k, cache_swa_v, w_in, ssd_conv_w, ssd_conv_b, ssd_dt_bias, ssd_a_log, ssd_d, ssd_norm_g, lru_conv_w, lru_conv_b, lru_wa, lru_ba, lru_wx, lru_bx, lru_lambda, attn_sinks, w_out, ln1_g, ln1_b, ln2_g, ln2_b, ffn_w_gate, ffn_w_up, ffn_w_down, moe_router, moe_w_gate, moe_w_up, moe_w_down):
    p = dict(w_in=w_in, ssd_conv_w=ssd_conv_w, ssd_conv_b=ssd_conv_b, ssd_dt_bias=ssd_dt_bias,
             ssd_a_log=ssd_a_log, ssd_d=ssd_d, ssd_norm_g=ssd_norm_g, lru_conv_w=lru_conv_w,
             lru_conv_b=lru_conv_b, lru_wa=lru_wa, lru_ba=lru_ba, lru_wx=lru_wx, lru_bx=lru_bx,
             lru_lambda=lru_lambda, attn_sinks=attn_sinks, w_out=w_out, ln1_g=ln1_g, ln1_b=ln1_b,
             ln2_g=ln2_g, ln2_b=ln2_b, ffn_w_gate=ffn_w_gate, ffn_w_up=ffn_w_up,
             ffn_w_down=ffn_w_down, moe_router=moe_router, moe_w_gate=moe_w_gate,
             moe_w_up=moe_w_up, moe_w_down=moe_w_down)
    lws = [_prep_layer(l, p) for l in range(DEPTH)]
    y_prompt, p_st = _prompt_trunk(x_prompt, lws, p)
    y_sample, s_st = _sample_trunk(x_sample, lws, p, state_ssd, state_ssd_conv, state_lru, state_lru_conv,
                                   cache_swa_k, cache_swa_v)
    return (y_prompt, y_sample) + tuple(p_st) + tuple(s_st)
```

```python
import functools
import math

import jax
import jax.numpy as jnp
from jax import lax
from jax.experimental import pallas as pl
from jax.experimental.pallas import tpu as pltpu

F32 = jnp.float32
BF16 = jnp.bfloat16

D_MODEL = 1024
DEPTH = 2
CONV_WIDTH = 4
SSD_HEADS = 16
SSD_HEAD_DIM = 64
SSD_WIDTH = SSD_HEADS * SSD_HEAD_DIM
SSD_GROUPS = 2
SSD_HPG = SSD_HEADS // SSD_GROUPS
SSD_STATE = 64
SSD_GN = SSD_GROUPS * SSD_STATE
SSD_CONV_DIM = SSD_WIDTH + 2 * SSD_GN
LRU_WIDTH = 512
LRU_BLOCKS = 8
LRU_BLOCK_DIM = LRU_WIDTH // LRU_BLOCKS
LRU_C = 8.0
ATTN_Q_HEADS = 8
ATTN_KV_HEADS = 2
ATTN_GROUP = ATTN_Q_HEADS // ATTN_KV_HEADS
ATTN_HEAD_DIM = 64
ATTN_WIDTH = ATTN_Q_HEADS * ATTN_HEAD_DIM
KV_DIM = ATTN_KV_HEADS * ATTN_HEAD_DIM
WINDOW = 128
ATTN_SCALE = ATTN_HEAD_DIM ** -0.5
MIX_WIDTH = SSD_WIDTH + LRU_WIDTH + ATTN_WIDTH
N_EXPERTS = 8
TOP_K = 2
ALPHA = (2.0 * DEPTH) ** 0.25
LN_EPS = 1e-5
RMS_EPS = 1e-5

LANES = 128
SUBLANES = 8
VMEM_LIMIT_BYTES = 56 * 1024 * 1024

OFF_Z = 0
OFF_XBC = OFF_Z + SSD_WIDTH
OFF_LGATE = OFF_XBC + SSD_CONV_DIM
OFF_LX = OFF_LGATE + LRU_WIDTH
OFF_Q = OFF_LX + LRU_WIDTH
OFF_K = OFF_Q + ATTN_WIDTH
OFF_V = OFF_K + KV_DIM
OFF_DT = OFF_V + KV_DIM
PROJ_WIDTH = OFF_DT + LANES

CHUNK = 128
TOKEN_TILE = 512
PROJ_COL_TILE = PROJ_WIDTH
FF_TILE = 1792
TOKEN_TILE_SHAPE = (SUBLANES, LANES)
assert SUBLANES * LANES == D_MODEL


def _cparams(*sem):
    return pltpu.CompilerParams(dimension_semantics=sem, vmem_limit_bytes=VMEM_LIMIT_BYTES)


def _dot(a, b):
    return jnp.dot(a.astype(BF16), b.astype(BF16), preferred_element_type=F32)


def _dot_nt(a, b):
    return lax.dot_general(a.astype(BF16), b.astype(BF16), (((1,), (1,)), ((), ())),
                           preferred_element_type=F32)


def _split3(x):
    hi = x.astype(BF16)
    r1 = x - hi.astype(F32)
    mid = r1.astype(BF16)
    lo = (r1 - mid.astype(F32)).astype(BF16)
    return hi, mid, lo


def _select_dot_rhs(x, sel):
    hi, mid, lo = _split3(x)
    d = lambda t: jnp.dot(t, sel, preferred_element_type=F32)
    return d(hi) + d(mid) + d(lo)


def _select_dot_lhs(sel, x):
    hi, mid, lo = _split3(x)
    d = lambda t: jnp.dot(sel, t, preferred_element_type=F32)
    return d(hi) + d(mid) + d(lo)


def _silu(x):
    return x * jax.nn.sigmoid(x)


def _layer_norm(h, g, b):
    mu = jnp.mean(h, axis=-1, keepdims=True)
    hc = h - mu
    var = jnp.mean(hc * hc, axis=-1, keepdims=True)
    return hc * lax.rsqrt(var + LN_EPS) * g + b


def _inproj_kernel(x_ref, w_ref, o_ref):
    o_ref[...] = jnp.dot(x_ref[...].astype(BF16), w_ref[...], preferred_element_type=F32)


def _in_proj(x, w):
    m = x.shape[0]
    tm = min(TOKEN_TILE, m)
    grid = (PROJ_WIDTH // PROJ_COL_TILE, m // tm)
    return pl.pallas_call(
        _inproj_kernel,
        grid=grid,
        in_specs=[pl.BlockSpec((tm, D_MODEL), lambda j, i: (i, 0)),
                  pl.BlockSpec((D_MODEL, PROJ_COL_TILE), lambda j, i: (0, j))],
        out_specs=pl.BlockSpec((tm, PROJ_COL_TILE), lambda j, i: (i, j)),
        out_shape=jax.ShapeDtypeStruct((m, PROJ_WIDTH), F32),
        compiler_params=_cparams("arbitrary", "arbitrary"),
        name="in_proj",
    )(x, w)


def _outproj_ln_kernel(y_ref, x_ref, w_ref, g_ref, b_ref, o_ref):
    mix = jnp.dot(y_ref[...].astype(BF16), w_ref[...], preferred_element_type=F32)
    o_ref[...] = _layer_norm(ALPHA * x_ref[...] + mix, g_ref[...], b_ref[...])


def _out_proj_ln(y_all, x, w, g, b):
    m = x.shape[0]
    tm = min(TOKEN_TILE, m)
    row = lambda i: (i, 0)
    fix = lambda i: (0, 0)
    return pl.pallas_call(
        _outproj_ln_kernel,
        grid=(m // tm,),
        in_specs=[pl.BlockSpec((tm, MIX_WIDTH), row), pl.BlockSpec((tm, D_MODEL), row),
                  pl.BlockSpec((MIX_WIDTH, D_MODEL), fix),
                  pl.BlockSpec((1, D_MODEL), fix), pl.BlockSpec((1, D_MODEL), fix)],
        out_specs=pl.BlockSpec((tm, D_MODEL), row),
        out_shape=jax.ShapeDtypeStruct((m, D_MODEL), F32),
        compiler_params=_cparams("arbitrary"),
        name="out_proj_ln",
    )(y_all, x, w, g, b)


def _swiglu_kernel(te_ref, tv_ref, x_ref, wg_ref, wu_ref, wd_ref, g_ref, b_ref, o_ref, acc_ref, xb_ref, *, fuse_ln):
    i = pl.program_id(0)
    f = pl.program_id(1)
    nf = pl.num_programs(1)
    live = tv_ref[i] > 0

    @pl.when(live & (f == 0))
    def _():
        xb_ref[...] = x_ref[...].reshape(xb_ref.shape).astype(BF16)

    @pl.when(live)
    def _():
        xb = xb_ref[...]
        hg = jnp.dot(xb, wg_ref[0], preferred_element_type=F32)
        hu = jnp.dot(xb, wu_ref[0], preferred_element_type=F32)
        part = jnp.dot((_silu(hg) * hu).astype(BF16), wd_ref[0], preferred_element_type=F32)

        @pl.when(f == 0)
        def _():
            acc_ref[...] = part

        @pl.when(f > 0)
        def _():
            acc_ref[...] += part

    @pl.when(live & (f == nf - 1))
    def _():
        if fuse_ln:
            o_ref[...] = _layer_norm(ALPHA * x_ref[...] + acc_ref[...], g_ref[...], b_ref[...])
        else:
            o_ref[...] = acc_ref[...].reshape(o_ref.shape)

    @pl.when(jnp.logical_not(live) & (f == nf - 1))
    def _():
        o_ref[...] = jnp.zeros_like(o_ref)


def _swiglu(x, tile_expert, tile_live, wg, wu, wd, g, b, *, tm, fuse_ln):
    m = x.shape[0]
    assert m % tm == 0 and tile_expert.shape[0] == m // tm and tile_live.shape[0] == m // tm
    row_block = (tm,) + x.shape[1:]
    row_map = (lambda i, f, te, tv: (i, 0)) if x.ndim == 2 else (lambda i, f, te, tv: (i, 0, 0))
    d_ff = wg.shape[2]
    ff = FF_TILE if d_ff % FF_TILE == 0 else d_ff // 2
    assert d_ff % ff == 0 and ff % LANES == 0
    nf = d_ff // ff
    fsel = lambda f, tv_i: f * tv_i + (nf - 1) * (1 - tv_i)
    fix = lambda i, f, te, tv: (0, 0)
    grid_spec = pltpu.PrefetchScalarGridSpec(
        num_scalar_prefetch=2,
        grid=(m // tm, nf),
        in_specs=[pl.BlockSpec(row_block, row_map),
                  pl.BlockSpec((1, D_MODEL, ff), lambda i, f, te, tv: (te[i], 0, fsel(f, tv[i]))),
                  pl.BlockSpec((1, D_MODEL, ff), lambda i, f, te, tv: (te[i], 0, fsel(f, tv[i]))),
                  pl.BlockSpec((1, ff, D_MODEL), lambda i, f, te, tv: (te[i], fsel(f, tv[i]), 0)),
                  pl.BlockSpec((1, D_MODEL), fix), pl.BlockSpec((1, D_MODEL), fix)],
        out_specs=pl.BlockSpec(row_block, row_map),
        scratch_shapes=[pltpu.VMEM((tm, D_MODEL), F32), pltpu.VMEM((tm, D_MODEL), BF16)],
    )
    return pl.pallas_call(
        functools.partial(_swiglu_kernel, fuse_ln=fuse_ln),
        grid_spec=grid_spec,
        out_shape=jax.ShapeDtypeStruct(x.shape, F32),
        compiler_params=_cparams("arbitrary", "arbitrary"),
        name="swiglu_ln" if fuse_ln else "swiglu_grouped",
    )(tile_expert, tile_live, x, wg, wu, wd, g, b)


def _router_kernel(x_ref, wr_ref, idx_ref, gate_ref, xt_ref):
    x = x_ref[...]
    xt_ref[...] = x.reshape(xt_ref.shape)
    logits = jnp.dot(x.astype(BF16), wr_ref[...], preferred_element_type=F32)
    lane = lax.broadcasted_iota(jnp.int32, logits.shape, 1)
    logits = jnp.where(lane < N_EXPERTS, logits, -jnp.inf)
    v1 = jnp.max(logits, axis=-1, keepdims=True)
    i1 = jnp.min(jnp.where(logits == v1, lane, LANES), axis=-1, keepdims=True)
    rest = jnp.where(lane == i1, -jnp.inf, logits)
    v2 = jnp.max(rest, axis=-1, keepdims=True)
    i2 = jnp.min(jnp.where(rest == v2, lane, LANES), axis=-1, keepdims=True)
    e2 = jnp.exp(v2 - v1)
    den = 1.0 + e2
    idx_ref[...] = jnp.where(lane == 0, i1, jnp.where(lane == 1, i2, 0))
    gate_ref[...] = jnp.where(lane == 0, 1.0 / den, jnp.where(lane == 1, e2 / den, 0.0))


def _router(x, wr):
    m = x.shape[0]
    tm = min(TOKEN_TILE, m)
    row = lambda i: (i, 0)
    return pl.pallas_call(
        _router_kernel,
        grid=(m // tm,),
        in_specs=[pl.BlockSpec((tm, D_MODEL), row), pl.BlockSpec((D_MODEL, LANES), lambda i: (0, 0))],
        out_specs=[pl.BlockSpec((tm, LANES), row), pl.BlockSpec((tm, LANES), row),
                   pl.BlockSpec((tm,) + TOKEN_TILE_SHAPE, lambda i: (i, 0, 0))],
        out_shape=[jax.ShapeDtypeStruct((m, LANES), jnp.int32), jax.ShapeDtypeStruct((m, LANES), F32),
                   jax.ShapeDtypeStruct((m,) + TOKEN_TILE_SHAPE, F32)],
        compiler_params=_cparams("arbitrary"),
        name="router_top2",
    )(x, wr)


def _row_gather_kernel(idx_ref, src_ref, dst_ref, sem, *, rows):
    def start(r, carry):
        pltpu.make_async_copy(src_ref.at[pl.ds(idx_ref[0, 0, r], 1)], dst_ref.at[pl.ds(r, 1)], sem).start()
        return carry

    lax.fori_loop(0, rows, start, 0, unroll=8)
    pltpu.make_async_copy(src_ref.at[pl.ds(0, rows)], dst_ref, sem).wait()


def _row_gather(src, idx, rows):
    n_rows = idx.shape[0]
    assert n_rows % rows == 0
    steps = n_rows // rows
    return pl.pallas_call(
        functools.partial(_row_gather_kernel, rows=rows),
        grid=(steps,),
        in_specs=[pl.BlockSpec((1, 1, rows), lambda i: (i, 0, 0), memory_space=pltpu.SMEM),
                  pl.BlockSpec(memory_space=pl.ANY)],
        out_specs=pl.BlockSpec((rows,) + src.shape[1:], lambda i: (i, 0, 0)),
        out_shape=jax.ShapeDtypeStruct((n_rows,) + src.shape[1:], src.dtype),
        scratch_shapes=[pltpu.SemaphoreType.DMA],
        compiler_params=_cparams("arbitrary"),
        name="row_gather",
    )(idx.reshape(steps, 1, rows), src)


def _combine_ln_kernel(x_ref, y_ref, gate_ref, g_ref, b_ref, o_ref):
    gate = gate_ref[...]
    x = x_ref[...]
    f = gate[:, 0:1] * y_ref[:, 0].reshape(x.shape) + gate[:, 1:2] * y_ref[:, 1].reshape(x.shape)
    o_ref[...] = _layer_norm(ALPHA * x + f, g_ref[...], b_ref[...])


def _combine_ln(x, y_pairs, gates, g, b):
    m = x.shape[0]
    tm = min(TOKEN_TILE, m)
    row = lambda i: (i, 0)
    fix = lambda i: (0, 0)
    return pl.pallas_call(
        _combine_ln_kernel,
        grid=(m // tm,),
        in_specs=[pl.BlockSpec((tm, D_MODEL), row),
                  pl.BlockSpec((tm, TOP_K) + TOKEN_TILE_SHAPE, lambda i: (i, 0, 0, 0)),
                  pl.BlockSpec((tm, LANES), row),
                  pl.BlockSpec((1, D_MODEL), fix), pl.BlockSpec((1, D_MODEL), fix)],
        out_specs=pl.BlockSpec((tm, D_MODEL), row),
        out_shape=jax.ShapeDtypeStruct((m, D_MODEL), F32),
        compiler_params=_cparams("arbitrary"),
        name="moe_combine_ln",
    )(x, y_pairs, gates, g, b)


def _moe_ffn_ln(x, wr, wg, wu, wd, g, b):
    m = x.shape[0]
    tm = min(TOKEN_TILE, m)
    top_i, gates, x_tiles = _router(x, wr)
    e_flat = top_i[:, :TOP_K].reshape(-1)
    onehot = (e_flat[:, None] == jnp.arange(N_EXPERTS, dtype=jnp.int32)[None, :]).astype(jnp.int32)
    blk = min(LANES, TOP_K * m)
    inner = jnp.cumsum(onehot.reshape(-1, blk, N_EXPERTS), axis=1)
    block_tot = inner[:, blk - 1, :]
    running = (inner + (jnp.cumsum(block_tot, axis=0) - block_tot)[:, None, :]).reshape(-1, N_EXPERTS)
    rank = jnp.sum(running * onehot, axis=1) - 1
    counts = jnp.sum(onehot, axis=0)
    padded = ((counts + tm - 1) // tm) * tm
    ends = jnp.cumsum(padded)
    starts = ends - padded
    pos = starts[e_flat] + rank
    n_pad = TOP_K * m + N_EXPERTS * tm
    n_tiles = n_pad // tm
    row_token = jnp.zeros((n_pad,), jnp.int32).at[pos].set(jnp.arange(TOP_K * m, dtype=jnp.int32) // TOP_K)
    tile_start = jnp.arange(n_tiles, dtype=jnp.int32) * tm
    tile_expert = jnp.minimum(jnp.sum((tile_start[:, None] >= ends[None, :]).astype(jnp.int32), axis=1),
                              N_EXPERTS - 1).astype(jnp.int32)
    tile_live = (tile_start < ends[N_EXPERTS - 1]).astype(jnp.int32)
    xs = _row_gather(x_tiles, row_token, tm)
    ys = _swiglu(xs, tile_expert, tile_live, wg, wu, wd, g, b, tm=tm, fuse_ln=False)
    y_pairs = _row_gather(ys, pos.astype(jnp.int32), tm).reshape((m, TOP_K) + TOKEN_TILE_SHAPE)
    return _combine_ln(x, y_pairs, gates, g, b)


def _shift_rows(x, tail, sh):
    if sh == 0:
        return x
    row8 = lax.broadcasted_iota(jnp.int32, (SUBLANES, 1), 0)
    xr = pltpu.roll(x, sh, 0)
    head = jnp.where(row8 < sh, pltpu.roll(tail, sh, 0), xr[0:SUBLANES])
    return jnp.concatenate([head, xr[SUBLANES:]], axis=0)


def _causal_conv(x, tail, w_ref, b_ref):
    acc = b_ref[...] + _shift_rows(x, tail, CONV_WIDTH - 1) * w_ref[0:1, :]
    for j in range(1, CONV_WIDTH):
        acc = acc + _shift_rows(x, tail, CONV_WIDTH - 1 - j) * w_ref[j:j + 1, :]
    return acc


def _lru_gates(xl, wax_ref, bax_ref, lam_ref):
    gates = jnp.dot(xl.astype(BF16), wax_ref[...], preferred_element_type=F32) + bax_ref[...]
    r = jax.nn.sigmoid(gates[:, :LRU_WIDTH])
    i = jax.nn.sigmoid(gates[:, LRU_WIDTH:])
    log_a = -LRU_C * r * jax.nn.softplus(-lam_ref[...])
    a = jnp.exp(log_a)
    th = jnp.tanh(log_a)
    u = jnp.sqrt(-2.0 * th / (1.0 - th)) * (i * xl)
    return a, u


def _sink_softmax(s, sink):
    m = jnp.maximum(jnp.max(s, axis=-1, keepdims=True), sink)
    e = jnp.exp(s - m)
    den = jnp.sum(e, axis=-1, keepdims=True) + jnp.exp(sink - m)
    return e, den


def _mixer_prompt_kernel(sink_ref, proj_ref, cw_ref, cb_ref, dtb_ref, alog_ref, drow_ref, ng_ref, rexp_ref,
                         gmask_ref, lcw_ref, lcb_ref, wax_ref, bax_ref, lam_ref,
                         y_ref, s_out_ref, hl_out_ref,
                         xtail, ltail, s_scr, hl_scr, kprev, vprev):
    c = pl.program_id(1)
    nc = pl.num_programs(1)
    t = CHUNK

    @pl.when(c == 0)
    def _():
        xtail[...] = jnp.zeros_like(xtail)
        ltail[...] = jnp.zeros_like(ltail)
        s_scr[...] = jnp.zeros_like(s_scr)
        hl_scr[...] = jnp.zeros_like(hl_scr)
        kprev[...] = jnp.zeros_like(kprev)
        vprev[...] = jnp.zeros_like(vprev)

    row = lax.broadcasted_iota(jnp.int32, (t, 1), 0)
    col = lax.broadcasted_iota(jnp.int32, (1, t), 1)
    lane = lax.broadcasted_iota(jnp.int32, (1, LANES), 1)
    causal = row >= col

    z = proj_ref[:, OFF_Z:OFF_Z + SSD_WIDTH]
    xbc_raw = proj_ref[:, OFF_XBC:OFF_XBC + SSD_CONV_DIM]
    xbc = _silu(_causal_conv(xbc_raw, xtail[...], cw_ref, cb_ref))
    xtail[...] = xbc_raw[t - SUBLANES:t]
    xs = xbc[:, :SSD_WIDTH]
    bm = xbc[:, SSD_WIDTH:SSD_WIDTH + SSD_GN]
    cm = xbc[:, SSD_WIDTH + SSD_GN:]
    dt = jax.nn.softplus(proj_ref[:, OFF_DT:OFF_DT + LANES] + dtb_ref[...])
    da = dt * (-jnp.exp(alog_ref[...]))
    acum = _select_dot_lhs(causal.astype(BF16), da)
    acum_t = acum.T
    rexp = rexp_ref[...]
    acum_e = _select_dot_rhs(acum, rexp)
    dt_e = _select_dot_rhs(dt, rexp)
    xdt = xs * dt_e
    exp_e = jnp.exp(acum_e)
    last_e = acum_e[t - 1:t, :]
    xw = xdt * jnp.exp(last_e - acum_e)
    bmb = bm.astype(BF16)

    pairs = []
    for g in range(SSD_GROUPS):
        gsel = (lane >= g * SSD_STATE) & (lane < (g + 1) * SSD_STATE)
        cb = _dot_nt(jnp.where(gsel, cm, 0.0), bmb)
        for pc in range(SSD_HPG // 2):
            p2 = g * (SSD_HPG // 2) + pc
            ms = []
            for j in (2 * p2, 2 * p2 + 1):
                seg = acum[:, j:j + 1] - acum_t[j:j + 1, :]
                dec = jnp.exp(jnp.where(causal, seg, -jnp.inf))
                ms.append((cb * dec).astype(BF16))
            lhs = jnp.concatenate(ms, axis=1)
            xp = xdt[:, LANES * p2:LANES * (p2 + 1)]
            rhs = jnp.concatenate([jnp.where(lane < SSD_HEAD_DIM, xp, 0.0),
                                   jnp.where(lane >= SSD_HEAD_DIM, xp, 0.0)], axis=0).astype(BF16)
            pairs.append(jnp.dot(lhs, rhs, preferred_element_type=F32))
    y_diag = jnp.concatenate(pairs, axis=1)

    s_old = s_scr[...]
    y_off = _dot(cm, s_old) * exp_e
    btx = _dot(bm.T, xw)
    s_scr[...] = s_old * exp_e[t - 1:t, :] + jnp.where(gmask_ref[...] > 0.0, btx, 0.0)

    y = (y_diag + y_off + drow_ref[...] * xs) * _silu(z)
    half = SSD_WIDTH // SSD_GROUPS
    normed = []
    for g in range(SSD_GROUPS):
        seg = y[:, g * half:(g + 1) * half]
        normed.append(seg * lax.rsqrt(jnp.mean(seg * seg, axis=-1, keepdims=True) + RMS_EPS))
    y_ssd = jnp.concatenate(normed, axis=1) * ng_ref[...]

    lx_raw = proj_ref[:, OFF_LX:OFF_LX + LRU_WIDTH]
    xl = _causal_conv(lx_raw, ltail[...], lcw_ref, lcb_ref)
    ltail[...] = lx_raw[t - SUBLANES:t]
    a_s, u_s = _lru_gates(xl, wax_ref, bax_ref, lam_ref)
    d = 1
    while d < t:
        a_sh = jnp.where(row >= d, pltpu.roll(a_s, d, 0), 1.0)
        u_sh = jnp.where(row >= d, pltpu.roll(u_s, d, 0), 0.0)
        u_s = a_s * u_sh + u_s
        a_s = a_s * a_sh
        d *= 2
    h_seq = a_s * hl_scr[...] + u_s
    hl_scr[...] = h_seq[t - 1:t, :]
    y_lru = h_seq * jax.nn.gelu(proj_ref[:, OFF_LGATE:OFF_LGATE + LRU_WIDTH])

    k = proj_ref[:, OFF_K:OFF_K + KV_DIM]
    v = proj_ref[:, OFF_V:OFF_V + KV_DIM]
    kc = jnp.concatenate([kprev[...], k], axis=0).astype(BF16)
    vc = jnp.concatenate([vprev[...], v], axis=0).astype(BF16)
    kprev[...] = k
    vprev[...] = v
    si = lax.broadcasted_iota(jnp.int32, (1, 2 * t), 1)
    first_key = jnp.where(c > 0, 0, t)
    valid = (si > row) & (si <= row + t) & (si >= first_key)
    blocks = []
    for g in range(ATTN_GROUP):
        qb = proj_ref[:, OFF_Q + LANES * g:OFF_Q + LANES * (g + 1)]
        halves = []
        for kv in range(ATTN_KV_HEADS):
            hsel = (lane >= kv * ATTN_HEAD_DIM) & (lane < (kv + 1) * ATTN_HEAD_DIM)
            s = _dot_nt(jnp.where(hsel, qb, 0.0), kc) * ATTN_SCALE
            s = jnp.where(valid, s, -jnp.inf)
            e, den = _sink_softmax(s, sink_ref[kv * ATTN_GROUP + g])
            halves.append(jnp.dot((e / den).astype(BF16), vc, preferred_element_type=F32))
        blocks.append(jnp.where(lane < ATTN_HEAD_DIM, halves[0], halves[1]))
    y_attn = jnp.concatenate(blocks, axis=1)

    y_ref[...] = jnp.concatenate([y_ssd, y_lru, y_attn], axis=1).astype(y_ref.dtype)

    @pl.when(c == nc - 1)
    def _():
        s_out_ref[0] = s_scr[...]
        hl_out_ref[0] = hl_scr[...]


def _mixer_prompt(proj, lw, bsz, seq):
    nc = seq // CHUNK
    fix = lambda b, c: (0, 0)
    vec = lambda n: pl.BlockSpec((1, n), fix)
    in_specs = [
        pl.BlockSpec(memory_space=pltpu.SMEM),
        pl.BlockSpec((CHUNK, PROJ_WIDTH), lambda b, c: (b * nc + c, 0)),
        pl.BlockSpec((CONV_WIDTH, SSD_CONV_DIM), fix), vec(SSD_CONV_DIM),
        vec(LANES), vec(LANES), vec(SSD_WIDTH), vec(SSD_WIDTH),
        pl.BlockSpec((LANES, SSD_WIDTH), fix), pl.BlockSpec((SSD_GN, SSD_WIDTH), fix),
        pl.BlockSpec((CONV_WIDTH, LRU_WIDTH), fix), vec(LRU_WIDTH),
        pl.BlockSpec((LRU_WIDTH, 2 * LRU_WIDTH), fix), vec(2 * LRU_WIDTH), vec(LRU_WIDTH),
    ]
    out_specs = [
        pl.BlockSpec((CHUNK, MIX_WIDTH), lambda b, c: (b * nc + c, 0)),
        pl.BlockSpec((1, SSD_GN, SSD_WIDTH), lambda b, c: (b, 0, 0)),
        pl.BlockSpec((1, 1, LRU_WIDTH), lambda b, c: (b, 0, 0)),
    ]
    out_shape = [
        jax.ShapeDtypeStruct((bsz * seq, MIX_WIDTH), BF16),
        jax.ShapeDtypeStruct((bsz, SSD_GN, SSD_WIDTH), F32),
        jax.ShapeDtypeStruct((bsz, 1, LRU_WIDTH), F32),
    ]
    scratch = [
        pltpu.VMEM((SUBLANES, SSD_CONV_DIM), F32), pltpu.VMEM((SUBLANES, LRU_WIDTH), F32),
        pltpu.VMEM((SSD_GN, SSD_WIDTH), F32), pltpu.VMEM((1, LRU_WIDTH), F32),
        pltpu.VMEM((CHUNK, KV_DIM), F32), pltpu.VMEM((CHUNK, KV_DIM), F32),
    ]
    return pl.pallas_call(
        _mixer_prompt_kernel,
        grid=(bsz, nc),
        in_specs=in_specs,
        out_specs=out_specs,
        out_shape=out_shape,
        scratch_shapes=scratch,
        compiler_params=_cparams("arbitrary", "arbitrary"),
        name="mixer_prompt",
    )(lw['sinks'], proj, lw['ssd_conv_w'], lw['ssd_conv_b'], lw['dt_bias'], lw['a_log'], lw['d_row'],
      lw['norm_g'], lw['head_expand'], lw['group_mask'], lw['lru_conv_w'], lw['lru_conv_b'],
      lw['lru_wax'], lw['lru_bax'], lw['lru_lambda'])


DEC_ROWS = SUBLANES


def _conv_step(x, buf_ref, w_ref, b_ref):
    acc = b_ref[...] + buf_ref[0] * w_ref[0:1, :]
    for j in range(1, CONV_WIDTH - 1):
        acc = acc + buf_ref[j] * w_ref[j:j + 1, :]
    return acc + x * w_ref[CONV_WIDTH - 1:CONV_WIDTH, :]


def _mixer_decode_kernel(sink_ref, proj_ref, sbuf_ref, lbuf_ref, hl_ref, kc_ref, vc_ref, h_ref,
                         cw_ref, cb_ref, dtb_ref, alog_ref, drow_ref, ng_ref, rexp_ref,
                         lcw_ref, lcb_ref, wax_ref, bax_ref, lam_ref,
                         y_ref, sbuf_out, lbuf_out, hl_out, kc_out, vc_out, h_out):
    nb = DEC_ROWS
    row8 = lax.broadcasted_iota(jnp.int32, (nb, 1), 0)
    lane = lax.broadcasted_iota(jnp.int32, (1, LANES), 1)
    wrow = lax.broadcasted_iota(jnp.int32, (WINDOW, 1), 0)

    z = proj_ref[:, OFF_Z:OFF_Z + SSD_WIDTH]
    xbc_raw = proj_ref[:, OFF_XBC:OFF_XBC + SSD_CONV_DIM]
    xbc = _silu(_conv_step(xbc_raw, sbuf_ref, cw_ref, cb_ref))
    for j in range(CONV_WIDTH - 2):
        sbuf_out[j] = sbuf_ref[j + 1]
    sbuf_out[CONV_WIDTH - 2] = xbc_raw
    xs = xbc[:, :SSD_WIDTH]
    bm = xbc[:, SSD_WIDTH:SSD_WIDTH + SSD_GN]
    cm = xbc[:, SSD_WIDTH + SSD_GN:]
    dt = jax.nn.softplus(proj_ref[:, OFF_DT:OFF_DT + LANES] + dtb_ref[...])
    da = dt * (-jnp.exp(alog_ref[...]))
    rexp = rexp_ref[...]
    decay_e = jnp.exp(_select_dot_rhs(da, rexp))
    xdt = xs * _select_dot_rhs(dt, rexp)
    cols = jnp.concatenate([xdt, decay_e, jnp.zeros((LANES - 2 * nb, SSD_WIDTH), F32)], axis=0)
    cols_t = [cols[:, LANES * p2:LANES * (p2 + 1)].T for p2 in range(SSD_HEADS // 2)]
    bm_g = [bm[:, :SSD_STATE], pltpu.roll(bm, SSD_STATE, 1)[:, :SSD_STATE]]
    cm_g = [cm[:, :SSD_STATE], pltpu.roll(cm, SSD_STATE, 1)[:, :SSD_STATE]]
    half = SSD_WIDTH // SSD_GROUPS
    y_off = [jnp.zeros((nb, half), F32) for _ in range(SSD_GROUPS)]
    for bb in range(nb):
        for g in range(SSD_GROUPS):
            h_g = h_ref[bb, g * SSD_HPG:(g + 1) * SSD_HPG].reshape(half, SSD_STATE)
            y_off[g] = jnp.where(row8 == bb, _dot_nt(cm_g[g], h_g), y_off[g])
        for p2 in range(SSD_HEADS // 2):
            g = p2 // (SSD_HPG // 2)
            h_pair = h_ref[bb, 2 * p2:2 * p2 + 2].reshape(2 * SSD_HEAD_DIM, SSD_STATE)
            xcol = cols_t[p2][:, bb:bb + 1]
            dcol = cols_t[p2][:, nb + bb:nb + bb + 1]
            h_new = h_pair * dcol + xcol * bm_g[g][bb:bb + 1, :]
            h_out[bb, 2 * p2:2 * p2 + 2] = h_new.reshape(2, SSD_HEAD_DIM, SSD_STATE)
    cbv = cm * bm
    ys = []
    for g in range(SSD_GROUPS):
        gsel = (lane >= g * SSD_STATE) & (lane < (g + 1) * SSD_STATE)
        cb_g = jnp.sum(jnp.where(gsel, cbv, 0.0), axis=-1, keepdims=True)
        sl = slice(g * half, (g + 1) * half)
        ys.append(cb_g * xdt[:, sl] + y_off[g] * decay_e[:, sl])
    y = (jnp.concatenate(ys, axis=1) + drow_ref[...] * xs) * _silu(z)
    normed = []
    for g in range(SSD_GROUPS):
        seg = y[:, g * half:(g + 1) * half]
        normed.append(seg * lax.rsqrt(jnp.mean(seg * seg, axis=-1, keepdims=True) + RMS_EPS))
    y_ssd = jnp.concatenate(normed, axis=1) * ng_ref[...]

    lx_raw = proj_ref[:, OFF_LX:OFF_LX + LRU_WIDTH]
    xl = _conv_step(lx_raw, lbuf_ref, lcw_ref, lcb_ref)
    for j in range(CONV_WIDTH - 2):
        lbuf_out[j] = lbuf_ref[j + 1]
    lbuf_out[CONV_WIDTH - 2] = lx_raw
    a_s, u_s = _lru_gates(xl, wax_ref, bax_ref, lam_ref)
    h_lru = a_s * hl_ref[...] + u_s
    hl_out[...] = h_lru
    y_lru = h_lru * jax.nn.gelu(proj_ref[:, OFF_LGATE:OFF_LGATE + LRU_WIDTH])

    k_new = proj_ref[:, OFF_K:OFF_K + KV_DIM]
    v_new = proj_ref[:, OFF_V:OFF_V + KV_DIM]
    head_row_g = row8 // ATTN_KV_HEADS
    head_row_kv = row8 % ATTN_KV_HEADS
    lane_kv = lane // ATTN_HEAD_DIM
    sink_col = jnp.zeros((nb, 1), F32)
    for r in range(ATTN_Q_HEADS):
        sink_col = jnp.where(row8 == r, sink_ref[(r % ATTN_KV_HEADS) * ATTN_GROUP + r // ATTN_KV_HEADS], sink_col)
    y_blocks = [jnp.zeros((nb, LANES), F32) for _ in range(ATTN_GROUP)]
    for bb in range(nb):
        q_rows = jnp.zeros((ATTN_Q_HEADS, LANES), F32)
        for g in range(ATTN_GROUP):
            qb = proj_ref[bb:bb + 1, OFF_Q + LANES * g:OFF_Q + LANES * (g + 1)]
            q_rows = jnp.where((head_row_g == g) & (head_row_kv == lane_kv), qb, q_rows)
        kb = kc_ref[bb]
        vb = vc_ref[bb]
        kn = k_new[bb:bb + 1, :]
        vn = v_new[bb:bb + 1, :]
        s = _dot_nt(q_rows, kb) * ATTN_SCALE
        s = jnp.where(lane >= 1, s, -jnp.inf)
        s_new = jnp.sum(q_rows * kn, axis=-1, keepdims=True) * ATTN_SCALE
        m = jnp.maximum(jnp.maximum(jnp.max(s, axis=-1, keepdims=True), s_new), sink_col)
        e = jnp.exp(s - m)
        e_new = jnp.exp(s_new - m)
        den = jnp.sum(e, axis=-1, keepdims=True) + e_new + jnp.exp(sink_col - m)
        o = _dot(e / den, vb) + (e_new / den) * vn
        for g in range(ATTN_GROUP):
            blk = jnp.where(lane < ATTN_HEAD_DIM, o[2 * g:2 * g + 1, :], o[2 * g + 1:2 * g + 2, :])
            y_blocks[g] = jnp.where(row8 == bb, blk, y_blocks[g])
        kc_out[bb] = jnp.where(wrow == WINDOW - 1, kn, pltpu.roll(kb, WINDOW - 1, 0))
        vc_out[bb] = jnp.where(wrow == WINDOW - 1, vn, pltpu.roll(vb, WINDOW - 1, 0))
    y_attn = jnp.concatenate(y_blocks, axis=1)

    y_ref[...] = jnp.concatenate([y_ssd, y_lru, y_attn], axis=1)


def _mixer_decode(proj, lw, sbuf, lbuf, hl, kc, vc, h):
    bsz = proj.shape[0]
    nb = DEC_ROWS
    fix = lambda i: (0, 0)
    vec = lambda n: pl.BlockSpec((1, n), fix)
    rows = lambda n: pl.BlockSpec((nb, n), lambda i: (i, 0))
    buf = lambda n: pl.BlockSpec((CONV_WIDTH - 1, nb, n), lambda i: (0, i, 0))
    cache = pl.BlockSpec((nb, WINDOW, KV_DIM), lambda i: (i, 0, 0))
    state = pl.BlockSpec((nb, SSD_HEADS, SSD_HEAD_DIM, SSD_STATE), lambda i: (i, 0, 0, 0))
    in_specs = [
        pl.BlockSpec(memory_space=pltpu.SMEM),
        rows(PROJ_WIDTH), buf(SSD_CONV_DIM), buf(LRU_WIDTH), rows(LRU_WIDTH), cache, cache, state,
        pl.BlockSpec((CONV_WIDTH, SSD_CONV_DIM), fix), vec(SSD_CONV_DIM),
        vec(LANES), vec(LANES), vec(SSD_WIDTH), vec(SSD_WIDTH),
        pl.BlockSpec((LANES, SSD_WIDTH), fix),
        pl.BlockSpec((CONV_WIDTH, LRU_WIDTH), fix), vec(LRU_WIDTH),
        pl.BlockSpec((LRU_WIDTH, 2 * LRU_WIDTH), fix), vec(2 * LRU_WIDTH), vec(LRU_WIDTH),
    ]
    out_specs = [rows(MIX_WIDTH), buf(SSD_CONV_DIM), buf(LRU_WIDTH), rows(LRU_WIDTH), cache, cache, state]
    out_shape = [
        jax.ShapeDtypeStruct((bsz, MIX_WIDTH), F32),
        jax.ShapeDtypeStruct(sbuf.shape, F32), jax.ShapeDtypeStruct(lbuf.shape, F32),
        jax.ShapeDtypeStruct(hl.shape, F32), jax.ShapeDtypeStruct(kc.shape, F32),
        jax.ShapeDtypeStruct(vc.shape, F32), jax.ShapeDtypeStruct(h.shape, F32),
    ]
    return pl.pallas_call(
        _mixer_decode_kernel,
        grid=(bsz // nb,),
        in_specs=in_specs,
        out_specs=out_specs,
        out_shape=out_shape,
        compiler_params=_cparams("arbitrary"),
        name="mixer_decode",
    )(lw['sinks'], proj, sbuf, lbuf, hl, kc, vc, h,
      lw['ssd_conv_w'], lw['ssd_conv_b'], lw['dt_bias'], lw['a_log'], lw['d_row'], lw['norm_g'],
      lw['head_expand'], lw['lru_conv_w'], lw['lru_conv_b'], lw['lru_wax'], lw['lru_bax'], lw['lru_lambda'])


def _q_perm():
    g, kv, d = jnp.meshgrid(jnp.arange(ATTN_GROUP), jnp.arange(ATTN_KV_HEADS), jnp.arange(ATTN_HEAD_DIM),
                            indexing='ij')
    return (kv * ATTN_GROUP * ATTN_HEAD_DIM + g * ATTN_HEAD_DIM + d).reshape(-1)


def _block_diag(w):
    eye = jnp.eye(LRU_BLOCKS, dtype=w.dtype)
    return jnp.einsum('kij,kl->kilj', w, eye).reshape(LRU_WIDTH, LRU_WIDTH)


def _prep_layer(l, p):
    w_in = p['w_in'][l]
    o = 0
    segs = {}
    for name, size in (('z', SSD_WIDTH), ('xbc', SSD_CONV_DIM), ('dt', SSD_HEADS), ('lgate', LRU_WIDTH),
                       ('lx', LRU_WIDTH), ('q', ATTN_WIDTH), ('k', KV_DIM), ('v', KV_DIM)):
        segs[name] = w_in[:, o:o + size]
        o += size
    qp = _q_perm()
    w_in_p = jnp.concatenate(
        [segs['z'], segs['xbc'], segs['lgate'], segs['lx'], segs['q'][:, qp], segs['k'], segs['v'],
         jnp.pad(segs['dt'], ((0, 0), (0, LANES - SSD_HEADS)))], axis=1).astype(BF16)
    w_out = p['w_out'][l]
    attn_rows = w_out[SSD_WIDTH + LRU_WIDTH:]
    w_out_p = jnp.concatenate([w_out[:SSD_WIDTH + LRU_WIDTH], attn_rows[qp]], axis=0).astype(BF16)
    pad_h = lambda v: jnp.pad(v, (0, LANES - SSD_HEADS)).reshape(1, LANES)
    head_of_lane = jnp.arange(SSD_WIDTH) // SSD_HEAD_DIM
    head_expand = (jnp.arange(LANES)[:, None] == head_of_lane[None, :]).astype(BF16)
    group_mask = ((jnp.arange(SSD_GN)[:, None] // SSD_STATE) == (head_of_lane[None, :] // SSD_HPG)).astype(F32)
    return dict(
        w_in=w_in_p, w_out=w_out_p,
        ssd_conv_w=p['ssd_conv_w'][l], ssd_conv_b=p['ssd_conv_b'][l].reshape(1, -1),
        dt_bias=pad_h(p['ssd_dt_bias'][l]), a_log=pad_h(p['ssd_a_log'][l]),
        d_row=jnp.repeat(p['ssd_d'][l], SSD_HEAD_DIM).reshape(1, -1),
        norm_g=p['ssd_norm_g'][l].reshape(1, -1),
        head_expand=head_expand, group_mask=group_mask,
        lru_conv_w=p['lru_conv_w'][l], lru_conv_b=p['lru_conv_b'][l].reshape(1, -1),
        lru_wax=jnp.concatenate([_block_diag(p['lru_wa'][l]), _block_diag(p['lru_wx'][l])], axis=1).astype(BF16),
        lru_bax=jnp.concatenate([p['lru_ba'][l].reshape(-1), p['lru_bx'][l].reshape(-1)]).reshape(1, -1),
        lru_lambda=p['lru_lambda'][l].reshape(1, -1),
        sinks=p['attn_sinks'][l],
        channel=_prep_channel(l, p),
        ln1_g=p['ln1_g'][l].reshape(1, -1), ln1_b=p['ln1_b'][l].reshape(1, -1),
        ln2_g=p['ln2_g'][l].reshape(1, -1), ln2_b=p['ln2_b'][l].reshape(1, -1),
    )


def _prep_channel(l, p):
    i = l // 2
    if l % 2 == 0:
        return dict(wg=p['ffn_w_gate'][i][None].astype(BF16), wu=p['ffn_w_up'][i][None].astype(BF16),
                    wd=p['ffn_w_down'][i][None].astype(BF16))
    return dict(wr=jnp.pad(p['moe_router'][i], ((0, 0), (0, LANES - N_EXPERTS))).astype(BF16),
                wg=p['moe_w_gate'][i].astype(BF16), wu=p['moe_w_up'][i].astype(BF16),
                wd=p['moe_w_down'][i].astype(BF16))


def _channel_mixer(l, x, lw):
    cw = lw['channel']
    if l % 2 == 0:
        tm = min(TOKEN_TILE, x.shape[0])
        n_tiles = x.shape[0] // tm
        return _swiglu(x, jnp.zeros((n_tiles,), jnp.int32), jnp.ones((n_tiles,), jnp.int32),
                       cw['wg'], cw['wu'], cw['wd'], lw['ln2_g'], lw['ln2_b'], tm=tm, fuse_ln=True)
    return _moe_ffn_ln(x, cw['wr'], cw['wg'], cw['wu'], cw['wd'], lw['ln2_g'], lw['ln2_b'])


def _ssd_state_from_scratch_layout(s):
    bsz = s.shape[0]
    s6 = s.reshape(bsz, SSD_GROUPS, SSD_STATE, SSD_GROUPS, SSD_HPG, SSD_HEAD_DIM)
    diag = jnp.stack([s6[:, g, :, g] for g in range(SSD_GROUPS)], axis=1)
    return jnp.transpose(diag, (0, 1, 3, 4, 2)).reshape(bsz, SSD_HEADS, SSD_HEAD_DIM, SSD_STATE)


def _prompt_trunk(x_prompt, lws, p):
    bsz, seq, _ = x_prompt.shape
    x = x_prompt.reshape(bsz * seq, D_MODEL)
    states = [[] for _ in range(6)]
    for l in range(DEPTH):
        lw = lws[l]
        proj = _in_proj(x, lw['w_in'])
        y_all, s_fin, hl_fin = _mixer_prompt(proj, lw, bsz, seq)
        proj3 = proj.reshape(bsz, seq, PROJ_WIDTH)
        states[0].append(_ssd_state_from_scratch_layout(s_fin))
        states[1].append(proj3[:, seq - (CONV_WIDTH - 1):, OFF_XBC:OFF_XBC + SSD_CONV_DIM])
        states[2].append(hl_fin.reshape(bsz, LRU_WIDTH))
        states[3].append(proj3[:, seq - (CONV_WIDTH - 1):, OFF_LX:OFF_LX + LRU_WIDTH])
        states[4].append(proj3[:, seq - WINDOW:, OFF_K:OFF_K + KV_DIM].reshape(bsz, WINDOW, ATTN_KV_HEADS, ATTN_HEAD_DIM))
        states[5].append(proj3[:, seq - WINDOW:, OFF_V:OFF_V + KV_DIM].reshape(bsz, WINDOW, ATTN_KV_HEADS, ATTN_HEAD_DIM))
        x = _out_proj_ln(y_all, x, lw['w_out'], lw['ln1_g'], lw['ln1_b'])
        x = _channel_mixer(l, x, lw)
    return x.reshape(bsz, seq, D_MODEL), [jnp.stack(s, axis=0) for s in states]


def _sample_trunk(x_sample, lws, p, state_ssd, state_ssd_conv, state_lru, state_lru_conv, cache_k, cache_v):
    bsz = x_sample.shape[0]
    x = x_sample.reshape(bsz, D_MODEL)
    states = [[] for _ in range(6)]
    for l in range(DEPTH):
        lw = lws[l]
        proj = _in_proj(x, lw['w_in'])
        y_all, sbuf, lbuf, hl, kc, vc, h = _mixer_decode(
            proj, lw, jnp.transpose(state_ssd_conv[l], (1, 0, 2)), jnp.transpose(state_lru_conv[l], (1, 0, 2)),
            state_lru[l], cache_k[l].reshape(bsz, WINDOW, KV_DIM), cache_v[l].reshape(bsz, WINDOW, KV_DIM),
            state_ssd[l])
        states[0].append(h)
        states[1].append(jnp.transpose(sbuf, (1, 0, 2)))
        states[2].append(hl)
        states[3].append(jnp.transpose(lbuf, (1, 0, 2)))
        states[4].append(kc.reshape(bsz, WINDOW, ATTN_KV_HEADS, ATTN_HEAD_DIM))
        states[5].append(vc.reshape(bsz, WINDOW, ATTN_KV_HEADS, ATTN_HEAD_DIM))
        x = _out_proj_ln(y_all, x, lw['w_out'], lw['ln1_g'], lw['ln1_b'])
        x = _channel_mixer(l, x, lw)
    return x.reshape(bsz, 1, D_MODEL), [jnp.stack(s, axis=0) for s in states]


def kernel(x_prompt, x_sample, state_ssd, state_ssd_conv, state_lru, state_lru_conv, cache_swa_k, cache_swa_v, w_in, ssd_conv_w, ssd_conv_b, ssd_dt_bias, ssd_a_log, ssd_d, ssd_norm_g, lru_conv_w, lru_conv_b, lru_wa, lru_ba, lru_wx, lru_bx, lru_lambda, attn_sinks, w_out, ln1_g, ln1_b, ln2_g, ln2_b, ffn_w_gate, ffn_w_up, ffn_w_down, moe_router, moe_w_gate, moe_w_up, moe_w_down):
    p = dict(w_in=w_in, ssd_conv_w=ssd_conv_w, ssd_conv_b=ssd_conv_b, ssd_dt_bias=ssd_dt_bias,
             ssd_a_log=ssd_a_log, ssd_d=ssd_d, ssd_norm_g=ssd_norm_g, lru_conv_w=lru_conv_w,
             lru_conv_b=lru_conv_b, lru_wa=lru_wa, lru_ba=lru_ba, lru_wx=lru_wx, lru_bx=lru_bx,
             lru_lambda=lru_lambda, attn_sinks=attn_sinks, w_out=w_out, ln1_g=ln1_g, ln1_b=ln1_b,
             ln2_g=ln2_g, ln2_b=ln2_b, ffn_w_gate=ffn_w_gate, ffn_w_up=ffn_w_up,
             ffn_w_down=ffn_w_down, moe_router=moe_router, moe_w_gate=moe_w_gate,
             moe_w_up=moe_w_up, moe_w_down=moe_w_down)
    lws = [_prep_layer(l, p) for l in range(DEPTH)]
    y_prompt, p_st = _prompt_trunk(x_prompt, lws, p)
    y_sample, s_st = _sample_trunk(x_sample, lws, p, state_ssd, state_ssd_conv, state_lru, state_lru_conv,
                                   cache_swa_k, cache_swa_v)
    return (y_prompt, y_sample) + tuple(p_st) + tuple(s_st)
```
